```python
import math
import jax, jax.numpy as jnp
from jax import lax
import numpy as np

D_MODEL = 1024
BATCH = 2
SEQ = 8192
DEPTH = 2

CHUNK = 64
D_MIX = D_MODEL
D_SSM = D_MIX // 2
D_MLSTM = D_MIX - D_SSM
SSM_GROUP = 16
SSM_GROUPS = D_SSM // SSM_GROUP
SSM_STATE = 64
ML_HEADS = 4
ML_HEAD_DIM = D_MLSTM // ML_HEADS
CONV_WIDTH = 4
D_FF = 4 * D_MODEL
D_IN = D_SSM + 4 * D_MLSTM + 2 * ML_HEADS
EPS = 1e-6
DT_MIN = 1e-3
DT_MAX = 1e-1

kernel_name = "hybrid_s5_mlstm_parallel_heads"


def rmsnorm(x, g):
    xf = x.astype(jnp.float32)
    y = xf * lax.rsqrt(jnp.mean(jnp.square(xf), axis=-1, keepdims=True) + EPS)
    return (y * g.astype(jnp.float32)).astype(x.dtype)


def causal_depthwise_conv(x, w, b):
    c = x.shape[-1]
    y = lax.conv_general_dilated(
        x, w[:, None, :].astype(x.dtype), window_strides=(1,),
        padding=[(CONV_WIDTH - 1, 0)], dimension_numbers=("NWC", "WIO", "NWC"),
        feature_group_count=c)
    return y + b.astype(x.dtype)


def s5_mixer(u, a_re, a_im, log_dt, b_re, b_im, c_re, c_im, d, w_glu, b_glu, out_g):
    bsz, L, _ = u.shape
    f32 = jnp.float32
    uf = u.astype(f32).reshape(bsz, L, SSM_GROUPS, SSM_GROUP)
    a_re = a_re.astype(f32); a_im = a_im.astype(f32)
    dt = jnp.exp(log_dt.astype(f32))[:, None]
    mag = jnp.exp(a_re * dt)
    ab_re = mag * jnp.cos(a_im * dt)
    ab_im = mag * jnp.sin(a_im * dt)
    den = jnp.square(a_re) + jnp.square(a_im)
    zr = ab_re - 1.0
    s_re = (zr * a_re + ab_im * a_im) / den
    s_im = (ab_im * a_re - zr * a_im) / den
    b_re = b_re.astype(f32); b_im = b_im.astype(f32)
    bb_re = s_re[..., None] * b_re - s_im[..., None] * b_im
    bb_im = s_re[..., None] * b_im + s_im[..., None] * b_re
    bu_re = jnp.einsum('blgp,gnp->blgn', uf, bb_re)
    bu_im = jnp.einsum('blgp,gnp->blgn', uf, bb_im)
    aa_re = jnp.broadcast_to(ab_re, bu_re.shape)
    aa_im = jnp.broadcast_to(ab_im, bu_im.shape)

    def combine(left, right):
        a1r, a1i, b1r, b1i = left
        a2r, a2i, b2r, b2i = right
        ar = a1r * a2r - a1i * a2i
        ai = a1r * a2i + a1i * a2r
        br = a2r * b1r - a2i * b1i + b2r
        bi = a2r * b1i + a2i * b1r + b2i
        return ar, ai, br, bi

    _, _, x_re, x_im = lax.associative_scan(combine, (aa_re, aa_im, bu_re, bu_im), axis=1)
    y = (jnp.einsum('blgn,gpn->blgp', x_re, c_re.astype(f32))
         - jnp.einsum('blgn,gpn->blgp', x_im, c_im.astype(f32))
         + d.astype(f32) * uf)
    y = y.reshape(bsz, L, D_SSM)
    z = jax.nn.gelu(y)
    z = z * jax.nn.sigmoid(z @ w_glu.astype(f32) + b_glu.astype(f32))
    return rmsnorm(z, out_g).astype(u.dtype)


def mlstm_mixer(q_raw, k_raw, v, o_raw, i_raw, f_raw, conv_w, conv_b, b_i, b_f, norm_g):
    bsz, L, _ = v.shape
    nc = L // CHUNK
    f32 = jnp.float32
    qk = jax.nn.silu(causal_depthwise_conv(jnp.concatenate([q_raw, k_raw], -1), conv_w, conv_b))
    q, k = jnp.split(qk, 2, axis=-1)

    def heads(t):
        return t.astype(f32).reshape(bsz, nc, CHUNK, ML_HEADS, ML_HEAD_DIM).transpose(1, 0, 3, 2, 4)

    def gates(t):
        return t.astype(f32).reshape(bsz, nc, CHUNK, ML_HEADS).transpose(1, 0, 3, 2)

    qc = heads(q)
    kc = heads(k) * (1.0 / math.sqrt(ML_HEAD_DIM))
    vc = heads(v)
    ig = gates(i_raw + b_i.astype(i_raw.dtype))
    lf = gates(jax.nn.log_sigmoid((f_raw + b_f.astype(f_raw.dtype)).astype(f32)))
    mask = jnp.tril(jnp.ones((CHUNK, CHUNK), dtype=bool))

    def step(carry, xs):
        c_st, n_st, m_st = carry
        qb, kb, vb, ib, fb = xs
        bcum = jnp.cumsum(fb, axis=-1)
        dmat = bcum[..., :, None] - bcum[..., None, :] + ib[..., None, :]
        dmat = jnp.where(mask, dmat, -jnp.inf)
        inter = bcum + m_st[..., None]
        m_t = jnp.maximum(inter, jnp.max(dmat, axis=-1))
        s = jnp.einsum('bhtd,bhsd->bhts', qb, kb) * jnp.exp(dmat - m_t[..., None])
        sc = jnp.exp(inter - m_t)
        num = (jnp.einsum('bhts,bhsv->bhtv', s, vb)
               + sc[..., None] * jnp.einsum('bhtk,bhkv->bhtv', qb, c_st))
        den = jnp.sum(s, axis=-1) + sc * jnp.einsum('bhtk,bhk->bht', qb, n_st)
        h = num / jnp.maximum(jnp.abs(den), jnp.exp(-m_t))[..., None]
        b_last = bcum[..., -1]
        g = b_last[..., None] - bcum + ib
        m_new = jnp.maximum(b_last + m_st, jnp.max(g, axis=-1))
        decay = jnp.exp(b_last + m_st - m_new)
        wk = jnp.exp(g - m_new[..., None])
        c_new = decay[..., None, None] * c_st + jnp.einsum('bhs,bhsk,bhsv->bhkv', wk, kb, vb)
        n_new = decay[..., None] * n_st + jnp.einsum('bhs,bhsk->bhk', wk, kb)
        return (c_new, n_new, m_new), h

    init = (jnp.zeros((bsz, ML_HEADS, ML_HEAD_DIM, ML_HEAD_DIM), f32),
            jnp.zeros((bsz, ML_HEADS, ML_HEAD_DIM), f32),
            jnp.zeros((bsz, ML_HEADS), f32))
    _, h = lax.scan(step, init, (qc, kc, vc, ig, lf))
    h = h.transpose(1, 0, 3, 2, 4).reshape(bsz, L, ML_HEADS, ML_HEAD_DIM)
    o = jax.nn.sigmoid(o_raw.astype(f32)).reshape(bsz, L, ML_HEADS, ML_HEAD_DIM)
    h = o * h
    h = rmsnorm(h, norm_g.reshape(ML_HEADS, ML_HEAD_DIM))
    return h.reshape(bsz, L, D_MLSTM).astype(v.dtype)


def setup_inputs(seed: int = 0) -> dict:
    key = jax.random.key(seed)
    ks = jax.random.split(key, 26)
    f32 = jnp.float32
    G, N, P, H = SSM_GROUPS, SSM_STATE, SSM_GROUP, ML_HEADS

    def nrm(k, shape, scale):
        return jax.random.normal(k, shape, f32) * scale

    def gain(k, shape):
        return 1.0 + 0.01 * jax.random.normal(k, shape, f32)

    x = jax.random.normal(ks[0], (BATCH, SEQ, D_MODEL), f32)
    norm_mix_g = gain(ks[1], (DEPTH, D_MODEL))
    w_in = nrm(ks[2], (DEPTH, D_MODEL, D_IN), D_MODEL ** -0.5)
    s5_a_re = -0.5 + 0.01 * jax.random.normal(ks[3], (DEPTH, G, N), f32)
    s5_a_im = (math.pi * jnp.arange(N, dtype=f32))[None, None, :] + 0.01 * jax.random.normal(ks[4], (DEPTH, G, N), f32)
    s5_log_dt = jax.random.uniform(ks[5], (DEPTH, G), f32, math.log(DT_MIN), math.log(DT_MAX))
    s5_b_re = nrm(ks[6], (DEPTH, G, N, P), (2 * P) ** -0.5)
    s5_b_im = nrm(ks[7], (DEPTH, G, N, P), (2 * P) ** -0.5)
    s5_c_re = nrm(ks[8], (DEPTH, G, P, N), N ** -0.5)
    s5_c_im = nrm(ks[9], (DEPTH, G, P, N), N ** -0.5)
    s5_d = nrm(ks[10], (DEPTH, G, P), 1.0)
    s5_w_glu = nrm(ks[11], (DEPTH, D_SSM, D_SSM), D_SSM ** -0.5)
    s5_b_glu = nrm(ks[12], (DEPTH, D_SSM), 0.01)
    s5_out_g = gain(ks[13], (DEPTH, D_SSM))
    ml_conv_w = nrm(ks[14], (DEPTH, CONV_WIDTH, 2 * D_MLSTM), CONV_WIDTH ** -0.5)
    ml_conv_b = nrm(ks[15], (DEPTH, 2 * D_MLSTM), 0.01)
    ml_b_i = nrm(ks[16], (DEPTH, H), 0.1)
    ml_b_f = jnp.linspace(3.0, 6.0, H, dtype=f32)[None, :] + 0.01 * jax.random.normal(ks[17], (DEPTH, H), f32)
    ml_norm_g = gain(ks[18], (DEPTH, D_MLSTM))
    w_out = nrm(ks[19], (DEPTH, D_MIX, D_MODEL), D_MIX ** -0.5)
    norm_ffn_g = gain(ks[20], (DEPTH, D_MODEL))
    w_ff1 = nrm(ks[21], (DEPTH, D_MODEL, D_FF), D_MODEL ** -0.5)
    w_ff2 = nrm(ks[22], (DEPTH, D_FF, D_MODEL), D_FF ** -0.5)
    final_norm_g = gain(ks[23], (D_MODEL,))
    return {"x": x, "norm_mix_g": norm_mix_g, "w_in": w_in,
            "s5_a_re": s5_a_re, "s5_a_im": s5_a_im, "s5_log_dt": s5_log_dt,
            "s5_b_re": s5_b_re, "s5_b_im": s5_b_im, "s5_c_re": s5_c_re, "s5_c_im": s5_c_im,
            "s5_d": s5_d, "s5_w_glu": s5_w_glu, "s5_b_glu": s5_b_glu, "s5_out_g": s5_out_g,
            "ml_conv_w": ml_conv_w, "ml_conv_b": ml_conv_b, "ml_b_i": ml_b_i, "ml_b_f": ml_b_f,
            "ml_norm_g": ml_norm_g, "w_out": w_out, "norm_ffn_g": norm_ffn_g,
            "w_ff1": w_ff1, "w_ff2": w_ff2, "final_norm_g": final_norm_g}


def reference(x, norm_mix_g, w_in, s5_a_re, s5_a_im, s5_log_dt, s5_b_re, s5_b_im,
              s5_c_re, s5_c_im, s5_d, s5_w_glu, s5_b_glu, s5_out_g, ml_conv_w, ml_conv_b,
              ml_b_i, ml_b_f, ml_norm_g, w_out, norm_ffn_g, w_ff1, w_ff2, final_norm_g):
    sizes = [D_SSM, D_MLSTM, D_MLSTM, D_MLSTM, D_MLSTM, ML_HEADS, ML_HEADS]
    cuts = [int(c) for c in np.cumsum(sizes)[:-1]]
    for l in range(DEPTH):
        h = rmsnorm(x, norm_mix_g[l])
        p = h @ w_in[l]
        u, q_raw, k_raw, v, o_raw, i_raw, f_raw = jnp.split(p, cuts, axis=-1)
        y_ssm = s5_mixer(u, s5_a_re[l], s5_a_im[l], s5_log_dt[l], s5_b_re[l], s5_b_im[l],
                         s5_c_re[l], s5_c_im[l], s5_d[l], s5_w_glu[l], s5_b_glu[l], s5_out_g[l])
        y_ml = mlstm_mixer(q_raw, k_raw, v, o_raw, i_raw, f_raw, ml_conv_w[l], ml_conv_b[l],
                           ml_b_i[l], ml_b_f[l], ml_norm_g[l])
        x = x + jnp.concatenate([y_ssm, y_ml], axis=-1) @ w_out[l]
        h = rmsnorm(x, norm_ffn_g[l])
        x = x + jnp.square(jax.nn.relu(h @ w_ff1[l])) @ w_ff2[l]
    return rmsnorm(x, final_norm_g)
```

```python
import functools
import math

import jax
import jax.numpy as jnp
from jax import lax
from jax.experimental import pallas as pl
from jax.experimental.pallas import tpu as pltpu

EPS = 1e-6
SSM_GROUP = 16
SSM_STATE = 64
ML_HEADS = 4
CONV_WIDTH = 4

SUBLANES = 8
LANES = 128
V7X_VMEM_BYTES = 64 * 1024 * 1024
VMEM_LIMIT = V7X_VMEM_BYTES - 8 * 1024 * 1024

S5_TILE = 512
ML_TILE = 512
ML_CHUNK = LANES
FFN_TILE = 512
FFN_COL = 1024
S5_BLOCK_GROUPS = 8

F32 = jnp.float32
BF16 = jnp.bfloat16


def _rms(x, g):
    ms = jnp.mean(x * x, axis=-1, keepdims=True)
    return x * lax.rsqrt(ms + EPS) * g


def _sigmoid(x):
    return 1.0 / (1.0 + jnp.exp(-x))


def _log_sigmoid(x):
    return jnp.minimum(x, 0.0) - jnp.log(1.0 + jnp.exp(-jnp.abs(x)))


def _gelu_tanh(x):
    c = math.sqrt(2.0 / math.pi)
    return x * (0.5 * (1.0 + jnp.tanh(c * (x + 0.044715 * (x * x * x)))))


def _const_spec(shape):
    nd = len(shape)
    return pl.BlockSpec(shape, lambda *_: (0,) * nd, pipeline_mode=pl.Buffered(1))


def _s5_kernel(x_ref, g_ref, wu_ref, p_ref, pt_ref, bcat_ref, ar_ref, ai_ref, asr_ref,
               asi_ref, ccat_ref, d_ref, wglu_ref, bglu_ref, og_ref, y_ref,
               st_ref, carry_ref, *, seg):
    nblk = bcat_ref.shape[0]
    hw = bcat_ref.shape[2] // 2

    @pl.when(pl.program_id(1) == 0)
    def _():
        carry_ref[...] = jnp.zeros_like(carry_ref)

    h = _rms(x_ref[0], g_ref[...]).astype(BF16)
    u = jnp.dot(h, wu_ref[...], preferred_element_type=F32)
    up = jnp.dot(p_ref[...], u.astype(BF16), preferred_element_type=F32)
    upb = up.astype(BF16)
    gw = upb.shape[1] // nblk
    for j in range(nblk):
        st_ref[:, j * 2 * hw:(j + 1) * 2 * hw] = jnp.dot(
            upb[:, j * gw:(j + 1) * gw], bcat_ref[j], preferred_element_type=F32)

    for j in range(nblk):
        cr = slice(j * 2 * hw, j * 2 * hw + hw)
        ci = slice(j * 2 * hw + hw, (j + 1) * 2 * hw)
        blk = slice(j * hw, (j + 1) * hw)
        ar = jnp.broadcast_to(ar_ref[:, blk], (SUBLANES, hw))
        ai = jnp.broadcast_to(ai_ref[:, blk], (SUBLANES, hw))

        def local_scan(s, c, cr=cr, ci=ci, ar=ar, ai=ai):
            sr, si = c
            row = pl.multiple_of(s * SUBLANES, SUBLANES)
            nr = ar * sr - ai * si + st_ref[pl.ds(row, SUBLANES), cr]
            ni = ar * si + ai * sr + st_ref[pl.ds(row, SUBLANES), ci]
            st_ref[pl.ds(row, SUBLANES), cr] = nr
            st_ref[pl.ds(row, SUBLANES), ci] = ni
            return nr, ni

        zero = jnp.zeros((SUBLANES, hw), F32)
        fr, fi = lax.fori_loop(0, seg, local_scan, (zero, zero))

        asr = asr_ref[:, blk]
        asi = asi_ref[:, blk]
        rows_r = [carry_ref[0:1, blk]]
        rows_i = [carry_ref[1:2, blk]]
        for k in range(SUBLANES):
            pr, pi = rows_r[-1], rows_i[-1]
            rows_r.append(asr * pr - asi * pi + fr[k:k + 1])
            rows_i.append(asr * pi + asi * pr + fi[k:k + 1])
        carry_ref[0:1, blk] = rows_r[SUBLANES]
        carry_ref[1:2, blk] = rows_i[SUBLANES]
        c_r = jnp.concatenate(rows_r[:SUBLANES], axis=0)
        c_i = jnp.concatenate(rows_i[:SUBLANES], axis=0)

        def add_carry(s, c, cr=cr, ci=ci, ar=ar, ai=ai):
            wr, wi = c
            nwr = ar * wr - ai * wi
            nwi = ar * wi + ai * wr
            row = pl.multiple_of(s * SUBLANES, SUBLANES)
            st_ref[pl.ds(row, SUBLANES), cr] = st_ref[pl.ds(row, SUBLANES), cr] + nwr
            st_ref[pl.ds(row, SUBLANES), ci] = st_ref[pl.ds(row, SUBLANES), ci] + nwi
            return nwr, nwi

        lax.fori_loop(0, seg, add_carry, (c_r, c_i))

    ys = []
    for j in range(nblk):
        xb = st_ref[:, j * 2 * hw:(j + 1) * 2 * hw].astype(BF16)
        ys.append(jnp.dot(xb, ccat_ref[j], preferred_element_type=F32))
    y = jnp.concatenate(ys, axis=1) + d_ref[...] * up
    z = _gelu_tanh(y)
    gate = _sigmoid(jnp.dot(z.astype(BF16), wglu_ref[...], preferred_element_type=F32)
                    + bglu_ref[...])
    zn = _rms(z * gate, og_ref[...]).astype(BF16)
    y_ref[0] = jnp.dot(pt_ref[...], zn, preferred_element_type=F32).astype(BF16)


def _s5_params(a_re, a_im, log_dt, b_re, b_im, c_re, c_im, seg):
    dt = jnp.exp(log_dt)[:, None]
    mag = jnp.exp(a_re * dt)
    ab_re = mag * jnp.cos(a_im * dt)
    ab_im = mag * jnp.sin(a_im * dt)
    den = jnp.square(a_re) + jnp.square(a_im)
    zr = ab_re - 1.0
    s_re = (zr * a_re + ab_im * a_im) / den
    s_im = (ab_im * a_re - zr * a_im) / den
    bb_re = s_re[..., None] * b_re - s_im[..., None] * b_im
    bb_im = s_re[..., None] * b_im + s_im[..., None] * b_re
    mag_s = jnp.exp(a_re * dt * seg)
    as_re = mag_s * jnp.cos(a_im * dt * seg)
    as_im = mag_s * jnp.sin(a_im * dt * seg)

    g, n, p = bb_re.shape
    gb = S5_BLOCK_GROUPS
    nblk = g // gb
    eye = jnp.eye(gb, dtype=F32)

    def in_blocks(bb):
        t = bb.reshape(nblk, gb, n, p)
        m = jnp.einsum('jgnp,gh->jgphn', t, eye)
        return m.reshape(nblk, gb * p, gb * n)

    def out_blocks(cc):
        t = cc.reshape(nblk, gb, p, n)
        m = jnp.einsum('jgpn,gh->jgnhp', t, eye)
        return m.reshape(nblk, gb * n, gb * p)

    bcat = jnp.concatenate([in_blocks(bb_re), in_blocks(bb_im)], axis=2).astype(BF16)
    ccat = jnp.concatenate([out_blocks(c_re), -out_blocks(c_im)], axis=1).astype(BF16)
    flat = lambda v: v.reshape(1, g * n)
    return bcat, ccat, flat(ab_re), flat(ab_im), flat(as_re), flat(as_im)


def _s5_branch(x, g, wu, a_re, a_im, log_dt, b_re, b_im, c_re, c_im, d, w_glu, b_glu, out_g):
    bsz, seq, dm = x.shape
    tile = S5_TILE
    seg = tile // SUBLANES
    ds = wu.shape[1]
    bcat, ccat, ar, ai, asr, asi = _s5_params(a_re, a_im, log_dt, b_re, b_im, c_re, c_im, seg)
    nst = ar.shape[1]
    r = jnp.arange(tile)
    src = (r % SUBLANES) * seg + r // SUBLANES
    perm = (jnp.arange(tile)[None, :] == src[:, None]).astype(BF16)
    args = (x, g.reshape(1, dm), wu.astype(BF16), perm, perm.T, bcat, ar, ai, asr, asi, ccat,
            d.reshape(1, ds), w_glu.astype(BF16), b_glu.reshape(1, ds), out_g.reshape(1, ds))
    in_specs = [pl.BlockSpec((1, tile, dm), lambda b, t: (b, t, 0))]
    in_specs += [_const_spec(a.shape) for a in args[1:]]
    return pl.pallas_call(
        functools.partial(_s5_kernel, seg=seg),
        grid=(bsz, seq // tile),
        in_specs=in_specs,
        out_specs=pl.BlockSpec((1, tile, ds), lambda b, t: (b, t, 0)),
        out_shape=jax.ShapeDtypeStruct((bsz, seq, ds), BF16),
        scratch_shapes=[pltpu.VMEM((tile, 2 * nst), F32), pltpu.VMEM((2, nst), F32)],
        compiler_params=pltpu.CompilerParams(
            dimension_semantics=("arbitrary", "arbitrary"), vmem_limit_bytes=VMEM_LIMIT),
        name="s5_branch",
    )(*args)


def _mlstm_kernel(x_ref, g_ref, w_ref, wg_ref, gb_ref, cw_ref, cb_ref, ng_ref, y_ref,
                  xc_ref, q_ref, k_ref, v_ref, o_ref, h_ref, a_ref, caug_ref, m_ref):
    tile = x_ref.shape[1]
    dml = q_ref.shape[1]
    dh = dml // ML_HEADS
    chunk = ML_CHUNK
    nchunk = tile // chunk
    tail = SUBLANES

    @pl.when(pl.program_id(1) == 0)
    def _():
        xc_ref[0:tail, :] = jnp.zeros((tail, xc_ref.shape[1]), F32)
        caug_ref[...] = jnp.zeros_like(caug_ref)
        m_ref[...] = jnp.zeros_like(m_ref)

    h = _rms(x_ref[0], g_ref[...]).astype(BF16)
    p = jnp.dot(h, w_ref[...], preferred_element_type=F32)
    gt = lax.dot_general(wg_ref[...], h, (((1,), (1,)), ((), ())),
                         preferred_element_type=F32) + gb_ref[...]

    xc_ref[tail:, :] = p[:, :2 * dml]
    acc = cb_ref[...]
    for j in range(CONV_WIDTH):
        off = tail - (CONV_WIDTH - 1) + j
        acc = acc + cw_ref[j:j + 1, :] * xc_ref[off:off + tile, :]
    xc_ref[0:tail, :] = p[tile - tail:, :2 * dml]
    qk = acc * _sigmoid(acc)
    q_ref[...] = qk[:, :dml].astype(BF16)
    k_ref[...] = (qk[:, dml:] * (1.0 / math.sqrt(dh))).astype(BF16)
    v_ref[...] = p[:, 2 * dml:3 * dml].astype(BF16)
    o_ref[...] = p[:, 3 * dml:]

    lf = _log_sigmoid(gt)
    lane = lax.broadcasted_iota(jnp.int32, gt.shape, 1) % chunk
    bc = lf
    shift = 1
    while shift < chunk:
        bc = bc + jnp.where(lane >= shift, pltpu.roll(bc, shift, axis=1), 0.0)
        shift *= 2
    a_row = gt[0:ML_HEADS] - bc[ML_HEADS:]
    for c in range(nchunk):
        a_ref[c, 0:ML_HEADS, :] = a_row[:, c * chunk:(c + 1) * chunk]
        a_ref[c, ML_HEADS:, :] = lf[ML_HEADS:, c * chunk:(c + 1) * chunk]

    row_i = lax.broadcasted_iota(jnp.int32, (chunk, chunk), 0)
    col_i = lax.broadcasted_iota(jnp.int32, (chunk, chunk), 1)
    tril = col_i <= row_i
    eye = col_i == row_i
    ones_blk = jnp.ones((chunk, dh), BF16)
    neg_inf = jnp.float32(-jnp.inf)

    def chunk_step(c, carry):
        r0 = pl.multiple_of(c * chunk, chunk)
        ab = a_ref[c]
        for hd in range(ML_HEADS):
            sl = slice(hd * dh, (hd + 1) * dh)
            qh = q_ref[pl.ds(r0, chunk), sl]
            kh = k_ref[pl.ds(r0, chunk), sl]
            vh = v_ref[pl.ds(r0, chunk), sl]
            a_b = jnp.broadcast_to(ab[hd:hd + 1, :], (chunk, chunk))
            lf_b = jnp.broadcast_to(ab[ML_HEADS + hd:ML_HEADS + hd + 1, :], (chunk, chunk))
            m_prev = m_ref[hd:hd + 1, 0:1]
            mc = jnp.maximum(jnp.max(jnp.where(tril, a_b, neg_inf), axis=1, keepdims=True), m_prev)
            dmat = jnp.exp(jnp.where(tril, a_b - mc, neg_inf))
            bcol = jnp.sum(jnp.where(tril, lf_b, 0.0), axis=1, keepdims=True)
            acol = jnp.sum(jnp.where(eye, a_b, 0.0), axis=1, keepdims=True)
            m_last = mc[chunk - 1:chunk, :]
            b_last = bcol[chunk - 1:chunk, :]
            sc = jnp.exp(m_prev - mc)
            e = jnp.exp(-(bcol + mc))
            wk = jnp.exp(acol - m_last)
            decay = sc[chunk - 1:chunk, :]
            s = lax.dot_general(qh, kh, (((1,), (1,)), ((), ())),
                                preferred_element_type=F32) * dmat
            vaug = jnp.concatenate([vh, ones_blk], axis=1)
            caug = caug_ref[hd]
            na = (jnp.dot(s.astype(BF16), vaug, preferred_element_type=F32)
                  + sc * jnp.dot(qh, caug.astype(BF16), preferred_element_type=F32))
            hh = na[:, :dh] / jnp.maximum(jnp.abs(na[:, dh:]), e)
            h_ref[pl.ds(r0, chunk), sl] = hh
            kw = (kh.astype(F32) * wk).astype(BF16)
            upd = lax.dot_general(kw, vaug, (((0,), (0,)), ((), ())),
                                  preferred_element_type=F32)
            caug_ref[hd] = decay * caug + upd
            m_ref[hd:hd + 1, :] = jnp.broadcast_to(b_last + m_last, (1, m_ref.shape[1]))
        return carry

    lax.fori_loop(0, nchunk, chunk_step, 0)

    hv = _sigmoid(o_ref[...]) * h_ref[...]
    outs = []
    for hd in range(ML_HEADS):
        sl = slice(hd * dh, (hd + 1) * dh)
        outs.append(_rms(hv[:, sl], ng_ref[:, sl]))
    y_ref[0] = jnp.concatenate(outs, axis=1).astype(BF16)


def _mlstm_branch(x, g, w_qkvo, w_i, w_f, conv_w, conv_b, b_i, b_f, norm_g):
    bsz, seq, dm = x.shape
    tile = ML_TILE
    dml = w_qkvo.shape[1] // 4
    dh = dml // ML_HEADS
    wg = jnp.concatenate([w_i, w_f], axis=1).T.astype(BF16)
    gb = jnp.concatenate([b_i, b_f]).reshape(2 * ML_HEADS, 1)
    args = (x, g.reshape(1, dm), w_qkvo.astype(BF16), wg, gb, conv_w,
            conv_b.reshape(1, 2 * dml), norm_g.reshape(1, dml))
    in_specs = [pl.BlockSpec((1, tile, dm), lambda b, t: (b, t, 0))]
    in_specs += [_const_spec(a.shape) for a in args[1:]]
    return pl.pallas_call(
        _mlstm_kernel,
        grid=(bsz, seq // tile),
        in_specs=in_specs,
        out_specs=pl.BlockSpec((1, tile, dml), lambda b, t: (b, t, 0)),
        out_shape=jax.ShapeDtypeStruct((bsz, seq, dml), BF16),
        scratch_shapes=[
            pltpu.VMEM((tile + SUBLANES, 2 * dml), F32),
            pltpu.VMEM((tile, dml), BF16),
            pltpu.VMEM((tile, dml), BF16),
            pltpu.VMEM((tile, dml), BF16),
            pltpu.VMEM((tile, dml), F32),
            pltpu.VMEM((tile, dml), F32),
            pltpu.VMEM((tile // ML_CHUNK, 2 * ML_HEADS, ML_CHUNK), F32),
            pltpu.VMEM((ML_HEADS, dh, 2 * dh), F32),
            pltpu.VMEM((2 * ML_HEADS, LANES), F32),
        ],
        compiler_params=pltpu.CompilerParams(
            dimension_semantics=("arbitrary", "arbitrary"), vmem_limit_bytes=VMEM_LIMIT),
        name="mlstm_branch",
    )(*args)


def _ffn_kernel(x_ref, ys_ref, ym_ref, wos_ref, wom_ref, g_ref, w1_ref, w2_ref, fg_ref,
                o_ref, *, final):
    x1 = (x_ref[...]
          + jnp.dot(ys_ref[...], wos_ref[...], preferred_element_type=F32)
          + jnp.dot(ym_ref[...], wom_ref[...], preferred_element_type=F32))
    hn = _rms(x1, g_ref[...]).astype(BF16)
    o_ref[...] = x1
    dff = w1_ref.shape[1]
    for c in range(dff // FFN_COL):
        cs = slice(c * FFN_COL, (c + 1) * FFN_COL)
        a = jnp.maximum(jnp.dot(hn, w1_ref[:, cs], preferred_element_type=F32), 0.0)
        o_ref[...] += jnp.dot((a * a).astype(BF16), w2_ref[cs, :], preferred_element_type=F32)
    if final:
        o_ref[...] = _rms(o_ref[...], fg_ref[...])


def _ffn_block(x2, ys2, ym2, w_out, g, w1, w2, fg, final):
    n, dm = x2.shape
    ds = ys2.shape[1]
    tile = FFN_TILE
    args = (x2, ys2, ym2, w_out[:ds].astype(BF16), w_out[ds:].astype(BF16), g.reshape(1, dm),
            w1.astype(BF16), w2.astype(BF16), fg.reshape(1, dm))
    row = lambda w: pl.BlockSpec((tile, w), lambda i: (i, 0))
    in_specs = [row(dm), row(ds), row(ym2.shape[1])] + [_const_spec(a.shape) for a in args[3:]]
    return pl.pallas_call(
        functools.partial(_ffn_kernel, final=final),
        grid=(n // tile,),
        in_specs=in_specs,
        out_specs=row(dm),
        out_shape=jax.ShapeDtypeStruct((n, dm), F32),
        compiler_params=pltpu.CompilerParams(
            dimension_semantics=("arbitrary",), vmem_limit_bytes=VMEM_LIMIT),
        name="out_ffn",
    )(*args)


def kernel(x, norm_mix_g, w_in, s5_a_re, s5_a_im, s5_log_dt, s5_b_re, s5_b_im, s5_c_re, s5_c_im, s5_d, s5_w_glu, s5_b_glu, s5_out_g, ml_conv_w, ml_conv_b, ml_b_i, ml_b_f, ml_norm_g, w_out, norm_ffn_g, w_ff1, w_ff2, final_norm_g):
    bsz, seq, dm = x.shape
    depth = w_in.shape[0]
    ds = s5_w_glu.shape[1]
    dml = ml_norm_g.shape[1]
    for l in range(depth):
        wl = w_in[l]
        y_ssm = _s5_branch(x, norm_mix_g[l], wl[:, :ds], s5_a_re[l], s5_a_im[l], s5_log_dt[l],
                           s5_b_re[l], s5_b_im[l], s5_c_re[l], s5_c_im[l], s5_d[l],
                           s5_w_glu[l], s5_b_glu[l], s5_out_g[l])
        y_ml = _mlstm_branch(x, norm_mix_g[l], wl[:, ds:ds + 4 * dml],
                             wl[:, ds + 4 * dml:ds + 4 * dml + ML_HEADS],
                             wl[:, ds + 4 * dml + ML_HEADS:], ml_conv_w[l], ml_conv_b[l],
                             ml_b_i[l], ml_b_f[l], ml_norm_g[l])
        x = _ffn_block(x.reshape(bsz * seq, dm), y_ssm.reshape(bsz * seq, ds),
                       y_ml.reshape(bsz * seq, dml), w_out[l], norm_ffn_g[l], w_ff1[l],
                       w_ff2[l], final_norm_g, final=(l == depth - 1)).reshape(bsz, seq, dm)
    return x
```

```python
import functools
import math

import jax
import jax.numpy as jnp
from jax import lax
from jax.experimental import pallas as pl
from jax.experimental.pallas import tpu as pltpu

EPS = 1e-6
SSM_GROUP = 16
SSM_STATE = 64
ML_HEADS = 4
CONV_WIDTH = 4

SUBLANES = 8
LANES = 128
V7X_VMEM_BYTES = 64 * 1024 * 1024
VMEM_LIMIT = V7X_VMEM_BYTES - 8 * 1024 * 1024

S5_TILE = 512
ML_TILE = 512
ML_CHUNK = LANES
FFN_TILE = 512
FFN_COL = 1024
S5_BLOCK_GROUPS = 8
S5_SCAN_UNROLL = 8

F32 = jnp.float32
BF16 = jnp.bfloat16


def _rms(x, g):
    ms = jnp.mean(x * x, axis=-1, keepdims=True)
    return x * lax.rsqrt(ms + EPS) * g


def _sigmoid(x):
    return 1.0 / (1.0 + jnp.exp(-x))


def _log_sigmoid(x):
    return jnp.minimum(x, 0.0) - jnp.log(1.0 + jnp.exp(-jnp.abs(x)))


def _gelu_tanh(x):
    c = math.sqrt(2.0 / math.pi)
    return x * (0.5 * (1.0 + jnp.tanh(c * (x + 0.044715 * (x * x * x)))))


def _const_spec(shape):
    nd = len(shape)
    return pl.BlockSpec(shape, lambda *_: (0,) * nd, pipeline_mode=pl.Buffered(1))


def _s5_kernel(x_ref, g_ref, wu_ref, p_ref, pt_ref, bcat_ref, ar_ref, ai_ref, asr_ref,
               asi_ref, ccat_ref, d_ref, wglu_ref, bglu_ref, og_ref, y_ref,
               up_ref, bu_ref, xs_ref, xb_ref, carry_ref, *, seg):
    nblk = bcat_ref.shape[0]
    hw = bcat_ref.shape[2] // 2
    pair = 2 * SUBLANES

    @pl.when(pl.program_id(1) == 0)
    def _():
        carry_ref[...] = jnp.zeros_like(carry_ref)

    h = _rms(x_ref[0], g_ref[...]).astype(BF16)
    u = jnp.dot(h, wu_ref[...], preferred_element_type=F32)
    up_ref[...] = jnp.dot(p_ref[...], u.astype(BF16), preferred_element_type=F32).astype(BF16)
    gw = up_ref.shape[1] // nblk
    for j in range(nblk):
        bu_ref[:, j * 2 * hw:(j + 1) * 2 * hw] = jnp.dot(
            up_ref[:, j * gw:(j + 1) * gw], bcat_ref[j], preferred_element_type=F32)

    for j in range(nblk):
        cr = slice(j * 2 * hw, j * 2 * hw + hw)
        ci = slice(j * 2 * hw + hw, (j + 1) * 2 * hw)
        blk = slice(j * hw, (j + 1) * hw)
        ar = jnp.broadcast_to(ar_ref[:, blk], (SUBLANES, hw))
        ai = jnp.broadcast_to(ai_ref[:, blk], (SUBLANES, hw))

        def local_scan(s, c, cr=cr, ci=ci, ar=ar, ai=ai):
            sr, si = c
            row = pl.multiple_of(s * SUBLANES, SUBLANES)
            nr = ar * sr - ai * si + bu_ref[pl.ds(row, SUBLANES), cr]
            ni = ar * si + ai * sr + bu_ref[pl.ds(row, SUBLANES), ci]
            xs_ref[pl.ds(row, SUBLANES), cr] = nr
            xs_ref[pl.ds(row, SUBLANES), ci] = ni
            return nr, ni

        zero = jnp.zeros((SUBLANES, hw), F32)
        fr, fi = lax.fori_loop(0, seg, local_scan, (zero, zero), unroll=S5_SCAN_UNROLL)

        asr = asr_ref[:, blk]
        asi = asi_ref[:, blk]
        rows_r = [carry_ref[0:1, blk]]
        rows_i = [carry_ref[1:2, blk]]
        for k in range(SUBLANES):
            pr, pi = rows_r[-1], rows_i[-1]
            rows_r.append(asr * pr - asi * pi + fr[k:k + 1])
            rows_i.append(asr * pi + asi * pr + fi[k:k + 1])
        carry_ref[0:1, blk] = rows_r[SUBLANES]
        carry_ref[1:2, blk] = rows_i[SUBLANES]
        c_r = jnp.concatenate(rows_r[:SUBLANES], axis=0)
        c_i = jnp.concatenate(rows_i[:SUBLANES], axis=0)

        def add_carry(i, c, cr=cr, ci=ci, ar=ar, ai=ai):
            wr, wi = c
            r0 = pl.multiple_of(i * pair, pair)
            r1 = r0 + SUBLANES
            w1r = ar * wr - ai * wi
            w1i = ar * wi + ai * wr
            w2r = ar * w1r - ai * w1i
            w2i = ar * w1i + ai * w1r
            xr = jnp.concatenate([xs_ref[pl.ds(r0, SUBLANES), cr] + w1r,
                                  xs_ref[pl.ds(r1, SUBLANES), cr] + w2r], axis=0)
            xi = jnp.concatenate([xs_ref[pl.ds(r0, SUBLANES), ci] + w1i,
                                  xs_ref[pl.ds(r1, SUBLANES), ci] + w2i], axis=0)
            xb_ref[pl.ds(r0, pair), cr] = xr.astype(BF16)
            xb_ref[pl.ds(r0, pair), ci] = xi.astype(BF16)
            return w2r, w2i

        lax.fori_loop(0, seg // 2, add_carry, (c_r, c_i), unroll=S5_SCAN_UNROLL // 2)

    ys = []
    for j in range(nblk):
        ys.append(jnp.dot(xb_ref[:, j * 2 * hw:(j + 1) * 2 * hw], ccat_ref[j],
                          preferred_element_type=F32))
    y = jnp.concatenate(ys, axis=1) + d_ref[...] * up_ref[...].astype(F32)
    z = _gelu_tanh(y)
    gate = _sigmoid(jnp.dot(z.astype(BF16), wglu_ref[...], preferred_element_type=F32)
                    + bglu_ref[...])
    zn = _rms(z * gate, og_ref[...]).astype(BF16)
    y_ref[0] = jnp.dot(pt_ref[...], zn, preferred_element_type=F32).astype(BF16)


def _s5_params(a_re, a_im, log_dt, b_re, b_im, c_re, c_im, seg):
    dt = jnp.exp(log_dt)[:, None]
    mag = jnp.exp(a_re * dt)
    ab_re = mag * jnp.cos(a_im * dt)
    ab_im = mag * jnp.sin(a_im * dt)
    den = jnp.square(a_re) + jnp.square(a_im)
    zr = ab_re - 1.0
    s_re = (zr * a_re + ab_im * a_im) / den
    s_im = (ab_im * a_re - zr * a_im) / den
    bb_re = s_re[..., None] * b_re - s_im[..., None] * b_im
    bb_im = s_re[..., None] * b_im + s_im[..., None] * b_re
    mag_s = jnp.exp(a_re * dt * seg)
    as_re = mag_s * jnp.cos(a_im * dt * seg)
    as_im = mag_s * jnp.sin(a_im * dt * seg)

    g, n, p = bb_re.shape
    gb = S5_BLOCK_GROUPS
    nblk = g // gb
    eye = jnp.eye(gb, dtype=F32)

    def in_blocks(bb):
        t = bb.reshape(nblk, gb, n, p)
        m = jnp.einsum('jgnp,gh->jgphn', t, eye)
        return m.reshape(nblk, gb * p, gb * n)

    def out_blocks(cc):
        t = cc.reshape(nblk, gb, p, n)
        m = jnp.einsum('jgpn,gh->jgnhp', t, eye)
        return m.reshape(nblk, gb * n, gb * p)

    bcat = jnp.concatenate([in_blocks(bb_re), in_blocks(bb_im)], axis=2).astype(BF16)
    ccat = jnp.concatenate([out_blocks(c_re), -out_blocks(c_im)], axis=1).astype(BF16)
    flat = lambda v: v.reshape(1, g * n)
    return bcat, ccat, flat(ab_re), flat(ab_im), flat(as_re), flat(as_im)


def _s5_branch(x, g, wu, a_re, a_im, log_dt, b_re, b_im, c_re, c_im, d, w_glu, b_glu, out_g):
    bsz, seq, dm = x.shape
    tile = S5_TILE
    seg = tile // SUBLANES
    ds = wu.shape[1]
    bcat, ccat, ar, ai, asr, asi = _s5_params(a_re, a_im, log_dt, b_re, b_im, c_re, c_im, seg)
    nst = ar.shape[1]
    r = jnp.arange(tile)
    src = (r % SUBLANES) * seg + r // SUBLANES
    perm = (jnp.arange(tile)[None, :] == src[:, None]).astype(BF16)
    args = (x, g.reshape(1, dm), wu.astype(BF16), perm, perm.T, bcat, ar, ai, asr, asi, ccat,
            d.reshape(1, ds), w_glu.astype(BF16), b_glu.reshape(1, ds), out_g.reshape(1, ds))
    in_specs = [pl.BlockSpec((1, tile, dm), lambda b, t: (b, t, 0))]
    in_specs += [_const_spec(a.shape) for a in args[1:]]
    return pl.pallas_call(
        functools.partial(_s5_kernel, seg=seg),
        grid=(bsz, seq // tile),
        in_specs=in_specs,
        out_specs=pl.BlockSpec((1, tile, ds), lambda b, t: (b, t, 0)),
        out_shape=jax.ShapeDtypeStruct((bsz, seq, ds), BF16),
        scratch_shapes=[
            pltpu.VMEM((tile, ds), BF16),
            pltpu.VMEM((tile, 2 * nst), F32),
            pltpu.VMEM((tile, 2 * nst), F32),
            pltpu.VMEM((tile, 2 * nst), BF16),
            pltpu.VMEM((2, nst), F32),
        ],
        compiler_params=pltpu.CompilerParams(
            dimension_semantics=("arbitrary", "arbitrary"), vmem_limit_bytes=VMEM_LIMIT),
        name="s5_branch",
    )(*args)


def _mlstm_kernel(x_ref, g_ref, w_ref, wg_ref, gb_ref, cw_ref, cb_ref, ng_ref, y_ref,
                  xc_ref, q_ref, k_ref, v_ref, o_ref, h_ref, a_ref, caug_ref, m_ref):
    tile = x_ref.shape[1]
    dml = q_ref.shape[1]
    dh = dml // ML_HEADS
    chunk = ML_CHUNK
    nchunk = tile // chunk
    tail = SUBLANES

    @pl.when(pl.program_id(1) == 0)
    def _():
        xc_ref[0:tail, :] = jnp.zeros((tail, xc_ref.shape[1]), F32)
        caug_ref[...] = jnp.zeros_like(caug_ref)
        m_ref[...] = jnp.zeros_like(m_ref)

    h = _rms(x_ref[0], g_ref[...]).astype(BF16)
    p = jnp.dot(h, w_ref[...], preferred_element_type=F32)
    gt = lax.dot_general(wg_ref[...], h, (((1,), (1,)), ((), ())),
                         preferred_element_type=F32) + gb_ref[...]

    xc_ref[tail:, :] = p[:, :2 * dml]
    acc = cb_ref[...]
    for j in range(CONV_WIDTH):
        off = tail - (CONV_WIDTH - 1) + j
        acc = acc + cw_ref[j:j + 1, :] * xc_ref[off:off + tile, :]
    xc_ref[0:tail, :] = p[tile - tail:, :2 * dml]
    qk = acc * _sigmoid(acc)
    q_ref[...] = qk[:, :dml].astype(BF16)
    k_ref[...] = (qk[:, dml:] * (1.0 / math.sqrt(dh))).astype(BF16)
    v_ref[...] = p[:, 2 * dml:3 * dml].astype(BF16)
    o_ref[...] = p[:, 3 * dml:]

    lf = _log_sigmoid(gt)
    lane = lax.broadcasted_iota(jnp.int32, gt.shape, 1) % chunk
    neg_inf = jnp.float32(-jnp.inf)
    bc = lf
    shift = 1
    while shift < chunk:
        bc = bc + jnp.where(lane >= shift, pltpu.roll(bc, shift, axis=1), 0.0)
        shift *= 2
    a8 = gt - pltpu.roll(bc, ML_HEADS, axis=0)
    cm = a8
    shift = 1
    while shift < chunk:
        cm = jnp.maximum(cm, jnp.where(lane >= shift, pltpu.roll(cm, shift, axis=1), neg_inf))
        shift *= 2
    for c in range(nchunk):
        cs = slice(c * chunk, (c + 1) * chunk)
        a_ref[0, c] = a8[:, cs]
        a_ref[1, c] = cm[:, cs]
        a_ref[2, c] = bc[:, cs]

    row_i = lax.broadcasted_iota(jnp.int32, (chunk, chunk), 0)
    col_i = lax.broadcasted_iota(jnp.int32, (chunk, chunk), 1)
    tril = col_i <= row_i
    ones_blk = jnp.ones((chunk, dh), BF16)

    def col_bcast(row):
        return jnp.broadcast_to(row, (chunk, chunk)).T

    def chunk_step(c):
        r0 = c * chunk
        a_t, cm_t, bc_t = a_ref[0, c], a_ref[1, c], a_ref[2, c]
        for hd in range(ML_HEADS):
            sl = slice(hd * dh, (hd + 1) * dh)
            qh = q_ref[pl.ds(r0, chunk), sl]
            kh = k_ref[pl.ds(r0, chunk), sl]
            vh = v_ref[pl.ds(r0, chunk), sl]
            a_r = a_t[hd:hd + 1, :]
            cm_r = cm_t[hd:hd + 1, :]
            bc_r = bc_t[ML_HEADS + hd:ML_HEADS + hd + 1, :]
            m_prev = m_ref[hd:hd + 1, :]
            m_col = jnp.maximum(col_bcast(cm_r), m_prev)
            d_in = jnp.exp(jnp.where(tril, a_r - m_col, neg_inf))
            d_st = jnp.exp(m_prev - m_col)
            e = jnp.exp(-(col_bcast(bc_r) + m_col))
            s = lax.dot_general(qh, kh, (((1,), (1,)), ((), ())),
                                preferred_element_type=F32) * d_in
            lhs = jnp.concatenate([s, qh.astype(F32) * d_st], axis=1).astype(BF16)
            vaug = jnp.concatenate([vh, ones_blk], axis=1)
            caug = caug_ref[hd]
            rhs = jnp.concatenate([vaug, caug.astype(BF16)], axis=0)
            na = jnp.dot(lhs, rhs, preferred_element_type=F32)
            hh = na[:, :dh] / jnp.maximum(jnp.abs(na[:, dh:]), e)
            h_ref[pl.ds(r0, chunk), sl] = hh
            wk_r = d_in[chunk - 1:chunk, :]
            decay = d_st[chunk - 1:chunk, :]
            kwt = (kh.T.astype(F32) * wk_r).astype(BF16)
            upd = jnp.dot(kwt, vaug, preferred_element_type=F32)
            caug_ref[hd] = jnp.concatenate([decay, decay], axis=1) * caug + upd
            m_new = bc_r[:, chunk - 1:chunk] + jnp.maximum(cm_r[:, chunk - 1:chunk], m_prev)
            m_ref[hd:hd + 1, :] = m_new

    for c in range(nchunk):
        chunk_step(c)

    hv = _sigmoid(o_ref[...]) * h_ref[...]
    outs = []
    for hd in range(ML_HEADS):
        sl = slice(hd * dh, (hd + 1) * dh)
        outs.append(_rms(hv[:, sl], ng_ref[:, sl]))
    y_ref[0] = jnp.concatenate(outs, axis=1).astype(BF16)


def _mlstm_branch(x, g, w_qkvo, w_i, w_f, conv_w, conv_b, b_i, b_f, norm_g):
    bsz, seq, dm = x.shape
    tile = ML_TILE
    dml = w_qkvo.shape[1] // 4
    dh = dml // ML_HEADS
    wg = jnp.concatenate([w_i, w_f], axis=1).T.astype(BF16)
    gb = jnp.concatenate([b_i, b_f]).reshape(2 * ML_HEADS, 1)
    args = (x, g.reshape(1, dm), w_qkvo.astype(BF16), wg, gb, conv_w,
            conv_b.reshape(1, 2 * dml), norm_g.reshape(1, dml))
    in_specs = [pl.BlockSpec((1, tile, dm), lambda b, t: (b, t, 0))]
    in_specs += [_const_spec(a.shape) for a in args[1:]]
    return pl.pallas_call(
        _mlstm_kernel,
        grid=(bsz, seq // tile),
        in_specs=in_specs,
        out_specs=pl.BlockSpec((1, tile, dml), lambda b, t: (b, t, 0)),
        out_shape=jax.ShapeDtypeStruct((bsz, seq, dml), BF16),
        scratch_shapes=[
            pltpu.VMEM((tile + SUBLANES, 2 * dml), F32),
            pltpu.VMEM((tile, dml), BF16),
            pltpu.VMEM((tile, dml), BF16),
            pltpu.VMEM((tile, dml), BF16),
            pltpu.VMEM((tile, dml), F32),
            pltpu.VMEM((tile, dml), F32),
            pltpu.VMEM((3, tile // ML_CHUNK, 2 * ML_HEADS, ML_CHUNK), F32),
            pltpu.VMEM((ML_HEADS, dh, 2 * dh), F32),
            pltpu.VMEM((2 * ML_HEADS, LANES), F32),
        ],
        compiler_params=pltpu.CompilerParams(
            dimension_semantics=("arbitrary", "arbitrary"), vmem_limit_bytes=VMEM_LIMIT),
        name="mlstm_branch",
    )(*args)


def _ffn_kernel(x_ref, ys_ref, ym_ref, wos_ref, wom_ref, g_ref, w1_ref, w2_ref, fg_ref,
                o_ref, *, final):
    x1 = (x_ref[...]
          + jnp.dot(ys_ref[...], wos_ref[...], preferred_element_type=F32)
          + jnp.dot(ym_ref[...], wom_ref[...], preferred_element_type=F32))
    hn = _rms(x1, g_ref[...]).astype(BF16)
    o_ref[...] = x1
    dff = w1_ref.shape[1]
    for c in range(dff // FFN_COL):
        cs = slice(c * FFN_COL, (c + 1) * FFN_COL)
        a = jnp.maximum(jnp.dot(hn, w1_ref[:, cs], preferred_element_type=F32), 0.0)
        o_ref[...] += jnp.dot((a * a).astype(BF16), w2_ref[cs, :], preferred_element_type=F32)
    if final:
        o_ref[...] = _rms(o_ref[...], fg_ref[...])


def _ffn_block(x2, ys2, ym2, w_out, g, w1, w2, fg, final):
    n, dm = x2.shape
    ds = ys2.shape[1]
    tile = FFN_TILE
    args = (x2, ys2, ym2, w_out[:ds].astype(BF16), w_out[ds:].astype(BF16), g.reshape(1, dm),
            w1.astype(BF16), w2.astype(BF16), fg.reshape(1, dm))
    row = lambda w: pl.BlockSpec((tile, w), lambda i: (i, 0))
    in_specs = [row(dm), row(ds), row(ym2.shape[1])] + [_const_spec(a.shape) for a in args[3:]]
    return pl.pallas_call(
        functools.partial(_ffn_kernel, final=final),
        grid=(n // tile,),
        in_specs=in_specs,
        out_specs=row(dm),
        out_shape=jax.ShapeDtypeStruct((n, dm), F32),
        compiler_params=pltpu.CompilerParams(
            dimension_semantics=("arbitrary",), vmem_limit_bytes=VMEM_LIMIT),
        name="out_ffn",
    )(*args)


def kernel(x, norm_mix_g, w_in, s5_a_re, s5_a_im, s5_log_dt, s5_b_re, s5_b_im, s5_c_re, s5_c_im, s5_d, s5_w_glu, s5_b_glu, s5_out_g, ml_conv_w, ml_conv_b, ml_b_i, ml_b_f, ml_norm_g, w_out, norm_ffn_g, w_ff1, w_ff2, final_norm_g):
    bsz, seq, dm = x.shape
    depth = w_in.shape[0]
    ds = s5_w_glu.shape[1]
    dml = ml_norm_g.shape[1]
    for l in range(depth):
        wl = w_in[l]
        y_ssm = _s5_branch(x, norm_mix_g[l], wl[:, :ds], s5_a_re[l], s5_a_im[l], s5_log_dt[l],
                           s5_b_re[l], s5_b_im[l], s5_c_re[l], s5_c_im[l], s5_d[l],
                           s5_w_glu[l], s5_b_glu[l], s5_out_g[l])
        y_ml = _mlstm_branch(x, norm_mix_g[l], wl[:, ds:ds + 4 * dml],
                             wl[:, ds + 4 * dml:ds + 4 * dml + ML_HEADS],
                             wl[:, ds + 4 * dml + ML_HEADS:], ml_conv_w[l], ml_conv_b[l],
                             ml_b_i[l], ml_b_f[l], ml_norm_g[l])
        x = _ffn_block(x.reshape(bsz * seq, dm), y_ssm.reshape(bsz * seq, ds),
                       y_ml.reshape(bsz * seq, dml), w_out[l], norm_ffn_g[l], w_ff1[l],
                       w_ff2[l], final_norm_g, final=(l == depth - 1)).reshape(bsz, seq, dm)
    return x
```

```python
import functools
import math

import jax
import jax.numpy as jnp
from jax import lax
from jax.experimental import pallas as pl
from jax.experimental.pallas import tpu as pltpu

EPS = 1e-6
SSM_GROUP = 16
SSM_STATE = 64
ML_HEADS = 4
CONV_WIDTH = 4

SUBLANES = 8
LANES = 128
V7X_VMEM_BYTES = 64 * 1024 * 1024
VMEM_LIMIT = V7X_VMEM_BYTES - 8 * 1024 * 1024

MIX_TILE = 512
ML_CHUNK = LANES
FFN_TILE = 512
FFN_COL = 1024
S5_BLOCK_GROUPS = 8
S5_SCAN_UNROLL = 8
MIX_SCHED_FLAGS = None

F32 = jnp.float32
BF16 = jnp.bfloat16


def _rms(x, g):
    ms = jnp.mean(x * x, axis=-1, keepdims=True)
    return x * lax.rsqrt(ms + EPS) * g


def _sigmoid(x):
    return 1.0 / (1.0 + jnp.exp(-x))


def _log_sigmoid(x):
    return jnp.minimum(x, 0.0) - jnp.log(1.0 + jnp.exp(-jnp.abs(x)))


def _gelu_tanh(x):
    c = math.sqrt(2.0 / math.pi)
    return x * (0.5 * (1.0 + jnp.tanh(c * (x + 0.044715 * (x * x * x)))))


def _const_spec(shape):
    nd = len(shape)
    return pl.BlockSpec(shape, lambda *_: (0,) * nd, pipeline_mode=pl.Buffered(1))


def _s5_bu(j, up_ref, bcat_ref, bu_ref):
    hw2 = bcat_ref.shape[2]
    gw = up_ref.shape[1] // bcat_ref.shape[0]
    bu_ref[:, j * hw2:(j + 1) * hw2] = jnp.dot(
        up_ref[:, j * gw:(j + 1) * gw], bcat_ref[j], preferred_element_type=F32)


def _s5_scan(ar_ref, ai_ref, asr_ref, asi_ref, bu_ref, xs_ref, xb_ref, carry_ref, seg):
    nst = ar_ref.shape[1]
    nblk = nst // (S5_BLOCK_GROUPS * SSM_STATE)
    hw = nst // nblk
    pair = 2 * SUBLANES
    for j in range(nblk):
        cr = slice(j * 2 * hw, j * 2 * hw + hw)
        ci = slice(j * 2 * hw + hw, (j + 1) * 2 * hw)
        blk = slice(j * hw, (j + 1) * hw)
        ar = jnp.broadcast_to(ar_ref[:, blk], (SUBLANES, hw))
        ai = jnp.broadcast_to(ai_ref[:, blk], (SUBLANES, hw))

        def local_scan(s, c, cr=cr, ci=ci, ar=ar, ai=ai):
            sr, si = c
            row = pl.multiple_of(s * SUBLANES, SUBLANES)
            nr = ar * sr - ai * si + bu_ref[pl.ds(row, SUBLANES), cr]
            ni = ar * si + ai * sr + bu_ref[pl.ds(row, SUBLANES), ci]
            xs_ref[pl.ds(row, SUBLANES), cr] = nr
            xs_ref[pl.ds(row, SUBLANES), ci] = ni
            return nr, ni

        zero = jnp.zeros((SUBLANES, hw), F32)
        fr, fi = lax.fori_loop(0, seg, local_scan, (zero, zero), unroll=S5_SCAN_UNROLL)

        asr = asr_ref[:, blk]
        asi = asi_ref[:, blk]
        rows_r = [carry_ref[0:1, blk]]
        rows_i = [carry_ref[1:2, blk]]
        for k in range(SUBLANES):
            pr, pi = rows_r[-1], rows_i[-1]
            rows_r.append(asr * pr - asi * pi + fr[k:k + 1])
            rows_i.append(asr * pi + asi * pr + fi[k:k + 1])
        carry_ref[0:1, blk] = rows_r[SUBLANES]
        carry_ref[1:2, blk] = rows_i[SUBLANES]
        c_r = jnp.concatenate(rows_r[:SUBLANES], axis=0)
        c_i = jnp.concatenate(rows_i[:SUBLANES], axis=0)

        def add_carry(i, c, cr=cr, ci=ci, ar=ar, ai=ai):
            wr, wi = c
            r0 = pl.multiple_of(i * pair, pair)
            r1 = r0 + SUBLANES
            w1r = ar * wr - ai * wi
            w1i = ar * wi + ai * wr
            w2r = ar * w1r - ai * w1i
            w2i = ar * w1i + ai * w1r
            xr = jnp.concatenate([xs_ref[pl.ds(r0, SUBLANES), cr] + w1r,
                                  xs_ref[pl.ds(r1, SUBLANES), cr] + w2r], axis=0)
            xi = jnp.concatenate([xs_ref[pl.ds(r0, SUBLANES), ci] + w1i,
                                  xs_ref[pl.ds(r1, SUBLANES), ci] + w2i], axis=0)
            xb_ref[pl.ds(r0, pair), cr] = xr.astype(BF16)
            xb_ref[pl.ds(r0, pair), ci] = xi.astype(BF16)
            return w2r, w2i

        lax.fori_loop(0, seg // 2, add_carry, (c_r, c_i), unroll=S5_SCAN_UNROLL // 2)


def _s5_cproj(j, xb_ref, ccat_ref, ys_ref):
    hw2 = ccat_ref.shape[1]
    gw = ccat_ref.shape[2]
    ys_ref[:, j * gw:(j + 1) * gw] = jnp.dot(xb_ref[:, j * hw2:(j + 1) * hw2], ccat_ref[j],
                                             preferred_element_type=F32)


def _s5_back(ys_ref, up_ref, d_ref, wglu_ref, bglu_ref, og_ref, pt_ref):
    y = ys_ref[...] + d_ref[...] * up_ref[...].astype(F32)
    z = _gelu_tanh(y)
    gate = _sigmoid(jnp.dot(z.astype(BF16), wglu_ref[...], preferred_element_type=F32)
                    + bglu_ref[...])
    zn = _rms(z * gate, og_ref[...]).astype(BF16)
    return jnp.dot(pt_ref[...], zn, preferred_element_type=F32).astype(BF16)


def _s5_params(a_re, a_im, log_dt, b_re, b_im, c_re, c_im, seg):
    dt = jnp.exp(log_dt)[:, None]
    mag = jnp.exp(a_re * dt)
    ab_re = mag * jnp.cos(a_im * dt)
    ab_im = mag * jnp.sin(a_im * dt)
    den = jnp.square(a_re) + jnp.square(a_im)
    zr = ab_re - 1.0
    s_re = (zr * a_re + ab_im * a_im) / den
    s_im = (ab_im * a_re - zr * a_im) / den
    bb_re = s_re[..., None] * b_re - s_im[..., None] * b_im
    bb_im = s_re[..., None] * b_im + s_im[..., None] * b_re
    mag_s = jnp.exp(a_re * dt * seg)
    as_re = mag_s * jnp.cos(a_im * dt * seg)
    as_im = mag_s * jnp.sin(a_im * dt * seg)

    g, n, p = bb_re.shape
    gb = S5_BLOCK_GROUPS
    nblk = g // gb
    eye = jnp.eye(gb, dtype=F32)

    def in_blocks(bb):
        t = bb.reshape(nblk, gb, n, p)
        m = jnp.einsum('jgnp,gh->jgphn', t, eye)
        return m.reshape(nblk, gb * p, gb * n)

    def out_blocks(cc):
        t = cc.reshape(nblk, gb, p, n)
        m = jnp.einsum('jgpn,gh->jgnhp', t, eye)
        return m.reshape(nblk, gb * n, gb * p)

    bcat = jnp.concatenate([in_blocks(bb_re), in_blocks(bb_im)], axis=2).astype(BF16)
    ccat = jnp.concatenate([out_blocks(c_re), -out_blocks(c_im)], axis=1).astype(BF16)
    flat = lambda v: v.reshape(1, g * n)
    return bcat, ccat, flat(ab_re), flat(ab_im), flat(as_re), flat(as_im)


def _ml_conv(c, ncol, cw_ref, cb_ref, xc_ref, q_ref, k_ref):
    tile, dml = q_ref.shape
    dh = dml // ML_HEADS
    tail = SUBLANES
    cols = slice(c * ncol, (c + 1) * ncol)
    acc = cb_ref[:, cols]
    for j in range(CONV_WIDTH):
        off = tail - (CONV_WIDTH - 1) + j
        acc = acc + cw_ref[j:j + 1, cols] * xc_ref[c, off:off + tile, :]
    xc_ref[c, 0:tail, :] = xc_ref[c, tile:tile + tail, :]
    qk = acc * _sigmoid(acc)
    if (c + 1) * ncol <= dml:
        q_ref[:, cols] = qk.astype(BF16)
    else:
        k_ref[:, c * ncol - dml:(c + 1) * ncol - dml] = (qk * (1.0 / math.sqrt(dh))).astype(BF16)


def _ml_gate_rows(gt, a_ref):
    tile = gt.shape[1]
    chunk = ML_CHUNK
    lf = _log_sigmoid(gt)
    lane = lax.broadcasted_iota(jnp.int32, gt.shape, 1) % chunk
    neg_inf = jnp.float32(-jnp.inf)
    bc = lf
    shift = 1
    while shift < chunk:
        bc = bc + jnp.where(lane >= shift, pltpu.roll(bc, shift, axis=1), 0.0)
        shift *= 2
    a8 = gt - pltpu.roll(bc, ML_HEADS, axis=0)
    cm = a8
    shift = 1
    while shift < chunk:
        cm = jnp.maximum(cm, jnp.where(lane >= shift, pltpu.roll(cm, shift, axis=1), neg_inf))
        shift *= 2
    for c in range(tile // chunk):
        cs = slice(c * chunk, (c + 1) * chunk)
        a_ref[0, c] = a8[:, cs]
        a_ref[1, c] = cm[:, cs]
        a_ref[2, c] = bc[:, cs]


def _ml_chunk(c, q_ref, k_ref, v_ref, a_ref, h_ref, caug_ref, m_ref):
    tile, dml = q_ref.shape
    dh = dml // ML_HEADS
    chunk = ML_CHUNK
    neg_inf = jnp.float32(-jnp.inf)
    row_i = lax.broadcasted_iota(jnp.int32, (chunk, chunk), 0)
    col_i = lax.broadcasted_iota(jnp.int32, (chunk, chunk), 1)
    tril = col_i <= row_i
    ones_blk = jnp.ones((chunk, dh), BF16)

    def col_bcast(row):
        return jnp.broadcast_to(row, (chunk, chunk)).T

    if True:
        rows = slice(c * chunk, (c + 1) * chunk)
        a_t, cm_t, bc_t = a_ref[0, c], a_ref[1, c], a_ref[2, c]
        for hd in range(ML_HEADS):
            sl = slice(hd * dh, (hd + 1) * dh)
            qh = q_ref[rows, sl]
            kh = k_ref[rows, sl]
            vh = v_ref[rows, sl]
            a_r = a_t[hd:hd + 1, :]
            cm_r = cm_t[hd:hd + 1, :]
            bc_r = bc_t[ML_HEADS + hd:ML_HEADS + hd + 1, :]
            m_prev = m_ref[hd:hd + 1, :]
            m_col = jnp.maximum(col_bcast(cm_r), m_prev)
            d_in = jnp.exp(jnp.where(tril, a_r - m_col, neg_inf))
            d_st = jnp.exp(m_prev - m_col)
            e = jnp.exp(-(col_bcast(bc_r) + m_col))
            s = lax.dot_general(qh, kh, (((1,), (1,)), ((), ())),
                                preferred_element_type=F32) * d_in
            lhs = jnp.concatenate([s, qh.astype(F32) * d_st], axis=1).astype(BF16)
            vaug = jnp.concatenate([vh, ones_blk], axis=1)
            caug = caug_ref[hd]
            rhs = jnp.concatenate([vaug, caug.astype(BF16)], axis=0)
            na = jnp.dot(lhs, rhs, preferred_element_type=F32)
            h_ref[rows, sl] = na[:, :dh] / jnp.maximum(jnp.abs(na[:, dh:]), e)
            wk_r = d_in[chunk - 1:chunk, :]
            decay = d_st[chunk - 1:chunk, :]
            kwt = (kh.T.astype(F32) * wk_r).astype(BF16)
            upd = jnp.dot(kwt, vaug, preferred_element_type=F32)
            caug_ref[hd] = jnp.concatenate([decay, decay], axis=1) * caug + upd
            m_ref[hd:hd + 1, :] = (bc_r[:, chunk - 1:chunk]
                                   + jnp.maximum(cm_r[:, chunk - 1:chunk], m_prev))


def _ml_back(o_ref, h_ref, ng_ref):
    dml = o_ref.shape[1]
    dh = dml // ML_HEADS
    hv = _sigmoid(o_ref[...]) * h_ref[...]
    outs = []
    for hd in range(ML_HEADS):
        sl = slice(hd * dh, (hd + 1) * dh)
        outs.append(_rms(hv[:, sl], ng_ref[:, sl]))
    return jnp.concatenate(outs, axis=1).astype(BF16)


def _mix_kernel(x_ref, g_ref, win_ref, wg_ref, gb_ref,
                p_ref, pt_ref, bcat_ref, ar_ref, ai_ref, asr_ref, asi_ref, ccat_ref, d_ref,
                wglu_ref, bglu_ref, og_ref, cw_ref, cb_ref, ng_ref,
                y_ref,
                hb_ref, up_ref, bu_ref, xs_ref, xb_ref, ys_ref, carry_ref,
                xc_ref, q_ref, k_ref, v_ref, o_ref, h_ref, a_ref, caug_ref, m_ref, *, seg):
    tile = hb_ref.shape[0]
    ds = up_ref.shape[1]
    dml = q_ref.shape[1]
    nblk = bcat_ref.shape[0]
    ncol = 2 * dml // nblk
    tail = SUBLANES

    @pl.when(pl.program_id(1) == 0)
    def _():
        carry_ref[...] = jnp.zeros_like(carry_ref)
        xc_ref[:, 0:tail, :] = jnp.zeros((nblk, tail, ncol), F32)
        caug_ref[...] = jnp.zeros_like(caug_ref)
        m_ref[...] = jnp.zeros_like(m_ref)

    def proj(lo, hi):
        return jnp.dot(hb_ref[...], win_ref[:, lo:hi], preferred_element_type=F32)

    def proj_qk(c):
        xc_ref[c, tail:, :] = proj(ds + c * ncol, ds + (c + 1) * ncol)

    hb_ref[...] = _rms(x_ref[0], g_ref[...]).astype(BF16)
    u = proj(0, ds)
    up_ref[...] = jnp.dot(p_ref[...], u.astype(BF16), preferred_element_type=F32).astype(BF16)
    gt = lax.dot_general(wg_ref[...], hb_ref[...], (((1,), (1,)), ((), ())),
                         preferred_element_type=F32) + gb_ref[...]
    _ml_gate_rows(gt, a_ref)
    proj_qk(0)
    for c in range(nblk):
        if c + 1 < nblk:
            proj_qk(c + 1)
        else:
            v_ref[...] = proj(ds + 2 * dml, ds + 3 * dml).astype(BF16)
        _ml_conv(c, ncol, cw_ref, cb_ref, xc_ref, q_ref, k_ref)
        _s5_bu(c, up_ref, bcat_ref, bu_ref)
    o_ref[...] = proj(ds + 3 * dml, ds + 4 * dml)

    _s5_scan(ar_ref, ai_ref, asr_ref, asi_ref, bu_ref, xs_ref, xb_ref, carry_ref, seg)

    for c in range(tile // ML_CHUNK):
        _ml_chunk(c, q_ref, k_ref, v_ref, a_ref, h_ref, caug_ref, m_ref)
        _s5_cproj(c, xb_ref, ccat_ref, ys_ref)
    y_ref[0, :, :ds] = _s5_back(ys_ref, up_ref, d_ref, wglu_ref, bglu_ref, og_ref, pt_ref)
    y_ref[0, :, ds:] = _ml_back(o_ref, h_ref, ng_ref)


def _mixer(x, g, w_in, s5p, mlp):
    bsz, seq, dm = x.shape
    tile = MIX_TILE
    seg = tile // SUBLANES
    (a_re, a_im, log_dt, b_re, b_im, c_re, c_im, d, w_glu, b_glu, out_g) = s5p
    (conv_w, conv_b, b_i, b_f, norm_g) = mlp
    ds = w_glu.shape[1]
    dml = norm_g.shape[0]
    dh = dml // ML_HEADS
    nmain = ds + 4 * dml
    bcat, ccat, ar, ai, asr, asi = _s5_params(a_re, a_im, log_dt, b_re, b_im, c_re, c_im, seg)
    nst = ar.shape[1]
    r = jnp.arange(tile)
    src = (r % SUBLANES) * seg + r // SUBLANES
    perm = (jnp.arange(tile)[None, :] == src[:, None]).astype(BF16)
    wg = w_in[:, nmain:].T.astype(BF16)
    gb = jnp.concatenate([b_i, b_f]).reshape(2 * ML_HEADS, 1)
    args = (x, g.reshape(1, dm), w_in[:, :nmain].astype(BF16), wg, gb,
            perm, perm.T, bcat, ar, ai, asr, asi, ccat, d.reshape(1, ds),
            w_glu.astype(BF16), b_glu.reshape(1, ds), out_g.reshape(1, ds),
            conv_w, conv_b.reshape(1, 2 * dml), norm_g.reshape(1, dml))
    in_specs = [pl.BlockSpec((1, tile, dm), lambda b, t: (b, t, 0))]
    in_specs += [_const_spec(a.shape) for a in args[1:]]
    return pl.pallas_call(
        functools.partial(_mix_kernel, seg=seg),
        grid=(bsz, seq // tile),
        in_specs=in_specs,
        out_specs=pl.BlockSpec((1, tile, ds + dml), lambda b, t: (b, t, 0)),
        out_shape=jax.ShapeDtypeStruct((bsz, seq, ds + dml), BF16),
        scratch_shapes=[
            pltpu.VMEM((tile, dm), BF16),
            pltpu.VMEM((tile, ds), BF16),
            pltpu.VMEM((tile, 2 * nst), F32),
            pltpu.VMEM((tile, 2 * nst), F32),
            pltpu.VMEM((tile, 2 * nst), BF16),
            pltpu.VMEM((tile, ds), F32),
            pltpu.VMEM((2, nst), F32),
            pltpu.VMEM((bcat.shape[0], tile + SUBLANES, 2 * dml // bcat.shape[0]), F32),
            pltpu.VMEM((tile, dml), BF16),
            pltpu.VMEM((tile, dml), BF16),
            pltpu.VMEM((tile, dml), BF16),
            pltpu.VMEM((tile, dml), F32),
            pltpu.VMEM((tile, dml), F32),
            pltpu.VMEM((3, tile // ML_CHUNK, 2 * ML_HEADS, ML_CHUNK), F32),
            pltpu.VMEM((ML_HEADS, dh, 2 * dh), F32),
            pltpu.VMEM((2 * ML_HEADS, LANES), F32),
        ],
        compiler_params=pltpu.CompilerParams(
            dimension_semantics=("arbitrary", "arbitrary"), vmem_limit_bytes=VMEM_LIMIT,
            flags=MIX_SCHED_FLAGS),
        name="mixer",
    )(*args)


def _ffn_kernel(x_ref, y_ref, wo_ref, g_ref, w1_ref, w2_ref, fg_ref, o_ref, *, final):
    x1 = x_ref[...] + jnp.dot(y_ref[...], wo_ref[...], preferred_element_type=F32)
    hn = _rms(x1, g_ref[...]).astype(BF16)
    o_ref[...] = x1
    dff = w1_ref.shape[1]
    for c in range(dff // FFN_COL):
        cs = slice(c * FFN_COL, (c + 1) * FFN_COL)
        a = jnp.maximum(jnp.dot(hn, w1_ref[:, cs], preferred_element_type=F32), 0.0)
        o_ref[...] += jnp.dot((a * a).astype(BF16), w2_ref[cs, :], preferred_element_type=F32)
    if final:
        o_ref[...] = _rms(o_ref[...], fg_ref[...])


def _ffn_block(x2, y2, w_out, g, w1, w2, fg, final):
    n, dm = x2.shape
    tile = FFN_TILE
    args = (x2, y2, w_out.astype(BF16), g.reshape(1, dm), w1.astype(BF16), w2.astype(BF16),
            fg.reshape(1, dm))
    row = lambda w: pl.BlockSpec((tile, w), lambda i: (i, 0))
    in_specs = [row(dm), row(y2.shape[1])] + [_const_spec(a.shape) for a in args[2:]]
    return pl.pallas_call(
        functools.partial(_ffn_kernel, final=final),
        grid=(n // tile,),
        in_specs=in_specs,
        out_specs=row(dm),
        out_shape=jax.ShapeDtypeStruct((n, dm), F32),
        compiler_params=pltpu.CompilerParams(
            dimension_semantics=("arbitrary",), vmem_limit_bytes=VMEM_LIMIT),
        name="out_ffn",
    )(*args)


def kernel(x, norm_mix_g, w_in, s5_a_re, s5_a_im, s5_log_dt, s5_b_re, s5_b_im, s5_c_re, s5_c_im, s5_d, s5_w_glu, s5_b_glu, s5_out_g, ml_conv_w, ml_conv_b, ml_b_i, ml_b_f, ml_norm_g, w_out, norm_ffn_g, w_ff1, w_ff2, final_norm_g):
    bsz, seq, dm = x.shape
    depth = w_in.shape[0]
    for l in range(depth):
        s5p = (s5_a_re[l], s5_a_im[l], s5_log_dt[l], s5_b_re[l], s5_b_im[l], s5_c_re[l],
               s5_c_im[l], s5_d[l], s5_w_glu[l], s5_b_glu[l], s5_out_g[l])
        mlp = (ml_conv_w[l], ml_conv_b[l], ml_b_i[l], ml_b_f[l], ml_norm_g[l])
        y = _mixer(x, norm_mix_g[l], w_in[l], s5p, mlp)
        x = _ffn_block(x.reshape(bsz * seq, dm), y.reshape(bsz * seq, y.shape[2]), w_out[l],
                       norm_ffn_g[l], w_ff1[l], w_ff2[l], final_norm_g,
                       final=(l == depth - 1)).reshape(bsz, seq, dm)
    return x
```

```python
import functools
import math

import jax
import jax.numpy as jnp
from jax import lax
from jax.experimental import pallas as pl
from jax.experimental.pallas import tpu as pltpu

EPS = 1e-6
SSM_GROUP = 16
SSM_STATE = 64
ML_HEADS = 4
CONV_WIDTH = 4

SUBLANES = 8
LANES = 128
V7X_VMEM_BYTES = 64 * 1024 * 1024
VMEM_LIMIT = V7X_VMEM_BYTES - 8 * 1024 * 1024

MIX_TILE = 512
MIX_ROW_BLOCK = 128
ML_CHUNK = LANES
FFN_TILE = 512
FFN_COL = 1024
S5_BLOCK_GROUPS = 8
S5_SCAN_PIECE = 16

F32 = jnp.float32
BF16 = jnp.bfloat16


def _rms(x, g):
    ms = jnp.mean(x * x, axis=-1, keepdims=True)
    return x * lax.rsqrt(ms + EPS) * g


def _sigmoid(x):
    return 1.0 / (1.0 + jnp.exp(-x))


def _log_sigmoid(x):
    return jnp.minimum(x, 0.0) - jnp.log(1.0 + jnp.exp(-jnp.abs(x)))


def _gelu_tanh(x):
    c = math.sqrt(2.0 / math.pi)
    return x * (0.5 * (1.0 + jnp.tanh(c * (x + 0.044715 * (x * x * x)))))


def _const_spec(shape):
    nd = len(shape)
    return pl.BlockSpec(shape, lambda *_: (0,) * nd, pipeline_mode=pl.Buffered(1))


def _run_interleaved(streams):
    totals = [sum(p[1] for p in s) for s in streams]
    pos = [0] * len(streams)
    spent = [0.0] * len(streams)
    done = set()
    while any(pos[i] < len(s) for i, s in enumerate(streams)):
        ready = [i for i, s in enumerate(streams)
                 if pos[i] < len(s) and all(r in done for r in s[pos[i]][2])]
        assert ready, "interleaving deadlock"
        i = min(ready, key=lambda k: spent[k] / totals[k])
        name, cost, _, fn = streams[i][pos[i]]
        fn()
        done.add(name)
        spent[i] += cost
        pos[i] += 1


def _s5_bu(j, rows, up_ref, bcat_ref, bu_ref):
    hw2 = bcat_ref.shape[2]
    gw = up_ref.shape[1] // bcat_ref.shape[0]
    bu_ref[j, rows, :] = jnp.dot(
        up_ref[rows, j * gw:(j + 1) * gw], bcat_ref[j], preferred_element_type=F32)


def _s5_scan_pieces(j, ar_ref, ai_ref, asr_ref, asi_ref, bu_ref, xs_ref, xb_ref, carry_ref, seg):
    nst = ar_ref.shape[1]
    hw = nst // (nst // (S5_BLOCK_GROUPS * SSM_STATE))
    cr = slice(0, hw)
    ci = slice(hw, 2 * hw)
    blk = slice(j * hw, (j + 1) * hw)
    pair = 2 * SUBLANES
    st = {}

    def coeffs():
        return (jnp.broadcast_to(ar_ref[:, blk], (SUBLANES, hw)),
                jnp.broadcast_to(ai_ref[:, blk], (SUBLANES, hw)))

    def pass1(s0):
        def fn():
            ar, ai = coeffs()
            sr, si = st.get("s", (jnp.zeros((SUBLANES, hw), F32),) * 2)
            for s in range(s0, s0 + S5_SCAN_PIECE):
                rows = slice(s * SUBLANES, (s + 1) * SUBLANES)
                sr, si = (ar * sr - ai * si + bu_ref[j, rows, cr],
                          ar * si + ai * sr + bu_ref[j, rows, ci])
                xs_ref[j, rows, cr] = sr
                xs_ref[j, rows, ci] = si
            st["s"] = (sr, si)
        return fn

    def carry():
        fr, fi = st["s"]
        asr = asr_ref[:, blk]
        asi = asi_ref[:, blk]
        rows_r = [carry_ref[0:1, blk]]
        rows_i = [carry_ref[1:2, blk]]
        for k in range(SUBLANES):
            pr, pi = rows_r[-1], rows_i[-1]
            rows_r.append(asr * pr - asi * pi + fr[k:k + 1])
            rows_i.append(asr * pi + asi * pr + fi[k:k + 1])
        carry_ref[0:1, blk] = rows_r[SUBLANES]
        carry_ref[1:2, blk] = rows_i[SUBLANES]
        st["w"] = (jnp.concatenate(rows_r[:SUBLANES], axis=0),
                   jnp.concatenate(rows_i[:SUBLANES], axis=0))

    def pass2(s0):
        def fn():
            ar, ai = coeffs()
            wr, wi = st["w"]
            for s in range(s0, s0 + S5_SCAN_PIECE, 2):
                r0 = slice(s * SUBLANES, (s + 1) * SUBLANES)
                r1 = slice((s + 1) * SUBLANES, (s + 2) * SUBLANES)
                w1r, w1i = ar * wr - ai * wi, ar * wi + ai * wr
                wr, wi = ar * w1r - ai * w1i, ar * w1i + ai * w1r
                both = slice(s * SUBLANES, (s + 2) * SUBLANES)
                xb_ref[j, both, cr] = jnp.concatenate(
                    [xs_ref[j, r0, cr] + w1r, xs_ref[j, r1, cr] + wr], axis=0).astype(BF16)
                xb_ref[j, both, ci] = jnp.concatenate(
                    [xs_ref[j, r0, ci] + w1i, xs_ref[j, r1, ci] + wi], axis=0).astype(BF16)
            st["w"] = (wr, wi)
        return fn

    pieces = []
    prev = []
    n = seg // S5_SCAN_PIECE
    for q in range(n):
        name = f"p1_{j}_{q}"
        pieces.append((name, 150, prev, pass1(q * S5_SCAN_PIECE)))
        prev = [name]
    pieces.append((f"carry{j}", 60, prev, carry))
    prev = [f"carry{j}"]
    for q in range(n):
        name = f"p2_{j}_{q}"
        pieces.append((name, 150, prev, pass2(q * S5_SCAN_PIECE)))
        prev = [name]
    return pieces, f"p1_{j}_{n - 1}", f"p2_{j}_{n - 1}"


def _s5_cproj(j, rows, xb_ref, ccat_ref, ys_ref):
    hw2 = ccat_ref.shape[1]
    gw = ccat_ref.shape[2]
    ys_ref[rows, j * gw:(j + 1) * gw] = jnp.dot(xb_ref[j, rows, :],
                                                ccat_ref[j], preferred_element_type=F32)


def _s5_back(rows, ys_ref, up_ref, d_ref, wglu_ref, bglu_ref, og_ref, zn_ref):
    y = ys_ref[rows, :] + d_ref[...] * up_ref[rows, :].astype(F32)
    z = _gelu_tanh(y)
    gate = _sigmoid(jnp.dot(z.astype(BF16), wglu_ref[...], preferred_element_type=F32)
                    + bglu_ref[...])
    zn_ref[rows, :] = _rms(z * gate, og_ref[...]).astype(BF16)


def _s5_params(a_re, a_im, log_dt, b_re, b_im, c_re, c_im, seg):
    dt = jnp.exp(log_dt)[:, None]
    mag = jnp.exp(a_re * dt)
    ab_re = mag * jnp.cos(a_im * dt)
    ab_im = mag * jnp.sin(a_im * dt)
    den = jnp.square(a_re) + jnp.square(a_im)
    zr = ab_re - 1.0
    s_re = (zr * a_re + ab_im * a_im) / den
    s_im = (ab_im * a_re - zr * a_im) / den
    bb_re = s_re[..., None] * b_re - s_im[..., None] * b_im
    bb_im = s_re[..., None] * b_im + s_im[..., None] * b_re
    mag_s = jnp.exp(a_re * dt * seg)
    as_re = mag_s * jnp.cos(a_im * dt * seg)
    as_im = mag_s * jnp.sin(a_im * dt * seg)

    g, n, p = bb_re.shape
    gb = S5_BLOCK_GROUPS
    nblk = g // gb
    eye = jnp.eye(gb, dtype=F32)

    def in_blocks(bb):
        t = bb.reshape(nblk, gb, n, p)
        m = jnp.einsum('jgnp,gh->jgphn', t, eye)
        return m.reshape(nblk, gb * p, gb * n)

    def out_blocks(cc):
        t = cc.reshape(nblk, gb, p, n)
        m = jnp.einsum('jgpn,gh->jgnhp', t, eye)
        return m.reshape(nblk, gb * n, gb * p)

    bcat = jnp.concatenate([in_blocks(bb_re), in_blocks(bb_im)], axis=2).astype(BF16)
    ccat = jnp.concatenate([out_blocks(c_re), -out_blocks(c_im)], axis=1).astype(BF16)
    flat = lambda v: v.reshape(1, g * n)
    return bcat, ccat, flat(ab_re), flat(ab_im), flat(as_re), flat(as_im)


def _ml_conv(c, part, nparts, ncol, cw_ref, cb_ref, xc_ref, q_ref, k_ref):
    tile, dml = q_ref.shape
    dh = dml // ML_HEADS
    tail = SUBLANES
    nrow = tile // nparts
    r0 = part * nrow
    cols = slice(c * ncol, (c + 1) * ncol)
    acc = cb_ref[:, cols]
    for j in range(CONV_WIDTH):
        off = r0 + tail - (CONV_WIDTH - 1) + j
        acc = acc + cw_ref[j:j + 1, cols] * xc_ref[c, off:off + nrow, :]
    if part == nparts - 1:
        xc_ref[c, 0:tail, :] = xc_ref[c, tile:tile + tail, :]
    qk = acc * _sigmoid(acc)
    rows = slice(r0, r0 + nrow)
    if (c + 1) * ncol <= dml:
        q_ref[rows, cols] = qk.astype(BF16)
    else:
        k_ref[rows, c * ncol - dml:(c + 1) * ncol - dml] = (
            qk * (1.0 / math.sqrt(dh))).astype(BF16)


def _ml_gate_rows(gt, a_ref):
    tile = gt.shape[1]
    chunk = ML_CHUNK
    lf = _log_sigmoid(gt)
    lane = lax.broadcasted_iota(jnp.int32, gt.shape, 1) % chunk
    neg_inf = jnp.float32(-jnp.inf)
    bc = lf
    shift = 1
    while shift < chunk:
        bc = bc + jnp.where(lane >= shift, pltpu.roll(bc, shift, axis=1), 0.0)
        shift *= 2
    a8 = gt - pltpu.roll(bc, ML_HEADS, axis=0)
    cm = a8
    shift = 1
    while shift < chunk:
        cm = jnp.maximum(cm, jnp.where(lane >= shift, pltpu.roll(cm, shift, axis=1), neg_inf))
        shift *= 2
    for c in range(tile // chunk):
        cs = slice(c * chunk, (c + 1) * chunk)
        a_ref[0, c] = a8[:, cs]
        a_ref[1, c] = cm[:, cs]
        a_ref[2, c] = bc[:, cs]


def _ml_chunk(c, q_ref, k_ref, v_ref, a_ref, h_ref, caug_ref, m_ref):
    tile, dml = q_ref.shape
    dh = dml // ML_HEADS
    chunk = ML_CHUNK
    neg_inf = jnp.float32(-jnp.inf)
    row_i = lax.broadcasted_iota(jnp.int32, (chunk, chunk), 0)
    col_i = lax.broadcasted_iota(jnp.int32, (chunk, chunk), 1)
    tril = col_i <= row_i
    ones_blk = jnp.ones((chunk, dh), BF16)

    def col_bcast(row):
        return jnp.broadcast_to(row, (chunk, chunk)).T

    rows = slice(c * chunk, (c + 1) * chunk)
    a_t, cm_t, bc_t = a_ref[0, c], a_ref[1, c], a_ref[2, c]
    for hd in range(ML_HEADS):
        sl = slice(hd * dh, (hd + 1) * dh)
        qh = q_ref[rows, sl]
        kh = k_ref[rows, sl]
        vh = v_ref[rows, sl]
        a_r = a_t[hd:hd + 1, :]
        cm_r = cm_t[hd:hd + 1, :]
        bc_r = bc_t[ML_HEADS + hd:ML_HEADS + hd + 1, :]
        m_prev = m_ref[hd:hd + 1, :]
        m_col = jnp.maximum(col_bcast(cm_r), m_prev)
        d_in = jnp.exp(jnp.where(tril, a_r - m_col, neg_inf))
        d_st = jnp.exp(m_prev - m_col)
        e = jnp.exp(-(col_bcast(bc_r) + m_col))
        s = lax.dot_general(qh, kh, (((1,), (1,)), ((), ())),
                            preferred_element_type=F32) * d_in
        lhs = jnp.concatenate([s, qh.astype(F32) * d_st], axis=1).astype(BF16)
        vaug = jnp.concatenate([vh, ones_blk], axis=1)
        caug = caug_ref[hd]
        rhs = jnp.concatenate([vaug, caug.astype(BF16)], axis=0)
        na = jnp.dot(lhs, rhs, preferred_element_type=F32)
        h_ref[rows, sl] = na[:, :dh] / jnp.maximum(jnp.abs(na[:, dh:]), e)
        wk_r = d_in[chunk - 1:chunk, :]
        decay = d_st[chunk - 1:chunk, :]
        kwt = (kh.T.astype(F32) * wk_r).astype(BF16)
        upd = jnp.dot(kwt, vaug, preferred_element_type=F32)
        caug_ref[hd] = jnp.concatenate([decay, decay], axis=1) * caug + upd
        m_ref[hd:hd + 1, :] = (bc_r[:, chunk - 1:chunk]
                               + jnp.maximum(cm_r[:, chunk - 1:chunk], m_prev))


def _ml_back(hd, o_ref, h_ref, ng_ref):
    dh = o_ref.shape[1] // ML_HEADS
    sl = slice(hd * dh, (hd + 1) * dh)
    hv = _sigmoid(o_ref[:, sl]) * h_ref[:, sl]
    return _rms(hv, ng_ref[:, sl]).astype(BF16)


def _mix_kernel(x_ref, g_ref, win_ref, wg_ref, gb_ref,
                p_ref, pt_ref, bcat_ref, ar_ref, ai_ref, asr_ref, asi_ref, ccat_ref, d_ref,
                wglu_ref, bglu_ref, og_ref, cw_ref, cb_ref, ng_ref,
                y_ref,
                hb_ref, ub_ref, up2_ref, bu_ref, xs_ref, xb_ref, ys_ref, zn_ref, carry_ref,
                xc_ref, q2_ref, k2_ref, v2_ref, o2_ref, h_ref, a2_ref, caug_ref, m_ref,
                *, seg, tiles_per_seq):
    tile = hb_ref.shape[0]
    ds = up2_ref.shape[2]
    dml = q2_ref.shape[2]
    nblk = bcat_ref.shape[0]
    ncol = 2 * dml // nblk
    tail = SUBLANES
    step = pl.program_id(0)
    front = step % 2
    back = 1 - front

    @pl.when(step == 0)
    def _():
        up2_ref[...] = jnp.zeros_like(up2_ref)
        bu_ref[...] = jnp.zeros_like(bu_ref)
        q2_ref[...] = jnp.zeros_like(q2_ref)
        k2_ref[...] = jnp.zeros_like(k2_ref)
        v2_ref[...] = jnp.zeros_like(v2_ref)
        o2_ref[...] = jnp.zeros_like(o2_ref)
        a2_ref[...] = jnp.zeros_like(a2_ref)
        carry_ref[...] = jnp.zeros_like(carry_ref)
        caug_ref[...] = jnp.zeros_like(caug_ref)
        m_ref[...] = jnp.zeros_like(m_ref)

    @pl.when(step % tiles_per_seq == 0)
    def _():
        xc_ref[:, 0:tail, :] = jnp.zeros((nblk, tail, ncol), F32)

    @pl.when(step % tiles_per_seq == 1 % tiles_per_seq)
    def _():
        carry_ref[...] = jnp.zeros_like(carry_ref)
        caug_ref[...] = jnp.zeros_like(caug_ref)
        m_ref[...] = jnp.zeros_like(m_ref)

    up_f, q_f, k_f, v_f, o_f, a_f = (r.at[front] for r in
                                     (up2_ref, q2_ref, k2_ref, v2_ref, o2_ref, a2_ref))
    up_b, q_b, k_b, v_b, o_b, a_b = (r.at[back] for r in
                                     (up2_ref, q2_ref, k2_ref, v2_ref, o2_ref, a2_ref))

    rb = MIX_ROW_BLOCK
    nrb = tile // rb
    rblocks = [slice(r * rb, (r + 1) * rb) for r in range(nrb)]

    def proj(rows, lo, hi):
        return jnp.dot(hb_ref[rows, :], win_ref[:, lo:hi], preferred_element_type=F32)

    def f_rms(rows):
        def fn():
            hb_ref[rows, :] = _rms(x_ref[0, rows, :], g_ref[...]).astype(BF16)
        return fn

    def f_u(rows):
        def fn():
            ub_ref[rows, :] = proj(rows, 0, ds).astype(BF16)
        return fn

    def f_perm(rows):
        def fn():
            up_f[rows, :] = jnp.dot(p_ref[rows, :], ub_ref[...],
                                    preferred_element_type=F32).astype(BF16)
        return fn

    def f_gates():
        gt = lax.dot_general(wg_ref[...], hb_ref[...], (((1,), (1,)), ((), ())),
                             preferred_element_type=F32) + gb_ref[...]
        _ml_gate_rows(gt, a_f)

    def f_qk(c, r):
        def fn():
            xc_ref[c, tail + r * rb:tail + (r + 1) * rb, :] = proj(
                rblocks[r], ds + c * ncol, ds + (c + 1) * ncol)
        return fn

    def f_v(rows):
        def fn():
            v_f[rows, :] = proj(rows, ds + 2 * dml, ds + 3 * dml).astype(BF16)
        return fn

    def f_o(rows):
        def fn():
            o_f[rows, :] = proj(rows, ds + 3 * dml, ds + 4 * dml)
        return fn

    scans = [_s5_scan_pieces(j, ar_ref, ai_ref, asr_ref, asi_ref, bu_ref, xs_ref, xb_ref,
                             carry_ref, seg) for j in range(nblk)]

    def b_unperm(rows):
        def fn():
            y_ref[0, rows, :ds] = jnp.dot(pt_ref[rows, :], zn_ref[...],
                                          preferred_element_type=F32).astype(BF16)
        return fn

    def b_ml(hd):
        def fn():
            dh = dml // ML_HEADS
            y_ref[0, :, ds + hd * dh:ds + (hd + 1) * dh] = _ml_back(hd, o_b, h_ref, ng_ref)
        return fn

    nchunk = tile // ML_CHUNK
    all_rms = [f"rms{r}" for r in range(nrb)]
    mxu_stream = [(f"u{r}", 128, [f"rms{r}"], f_u(rblocks[r])) for r in range(nrb)]
    mxu_stream += [(f"perm{r}", 64, [f"u{q}" for q in range(nrb)], f_perm(rblocks[r]))
                   for r in range(nrb)]
    mxu_stream += [(f"qk0_{r}", 128, [f"rms{r}"], f_qk(0, r)) for r in range(nrb)]
    for c in range(nblk):
        if c + 1 < nblk:
            mxu_stream += [(f"qk{c + 1}_{r}", 128, [f"rms{r}"], f_qk(c + 1, r))
                           for r in range(nrb)]
        mxu_stream += [(f"bu{c}_{r}", 150, [f"perm{r}", scans[c][1]], functools.partial(
            _s5_bu, c, rblocks[r], up_f, bcat_ref, bu_ref)) for r in range(nrb)]
        if c == 0:
            mxu_stream += [(f"v{r}", 128, [f"rms{r}"], f_v(rblocks[r])) for r in range(nrb)]
        if c == 1:
            mxu_stream += [(f"o{r}", 256, [f"rms{r}"], f_o(rblocks[r])) for r in range(nrb)]

    vpu_stream = [(f"rms{r}", 150, [], f_rms(rblocks[r])) for r in range(nrb)]
    vpu_stream.append(("gates", 300, all_rms, f_gates))
    for j in range(nblk):
        vpu_stream.extend(scans[j][0])
        for part in range(nrb):
            vpu_stream.append((f"conv{j}_{part}", 275, [f"qk{j}_{part}"], functools.partial(
                _ml_conv, j, part, nrb, ncol, cw_ref, cb_ref, xc_ref, q_f, k_f)))
    _run_interleaved([mxu_stream, vpu_stream])

    chunk_stream = [(f"chunk{c}", 1500, [], functools.partial(
        _ml_chunk, c, q_b, k_b, v_b, a_b, h_ref, caug_ref, m_ref)) for c in range(nchunk)]
    chunk_stream += [(f"mlback{hd}", 300, [], b_ml(hd)) for hd in range(ML_HEADS)]
    out_stream = [(f"cproj{c}_{r}", 130, [], functools.partial(
        _s5_cproj, c, rblocks[r], xb_ref, ccat_ref, ys_ref))
        for c in range(nblk) for r in range(nrb)]
    out_stream += [(f"s5back{r}", 350, [], functools.partial(
        _s5_back, rblocks[r], ys_ref, up_b, d_ref, wglu_ref, bglu_ref, og_ref, zn_ref))
        for r in range(nrb)]
    out_stream += [(f"unperm{r}", 64, [], b_unperm(rblocks[r])) for r in range(nrb)]
    _run_interleaved([chunk_stream, out_stream])


def _mixer(x, g, w_in, s5p, mlp):
    bsz, seq, dm = x.shape
    tile = MIX_TILE
    seg = tile // SUBLANES
    tiles_per_seq = seq // tile
    ntiles = bsz * tiles_per_seq
    (a_re, a_im, log_dt, b_re, b_im, c_re, c_im, d, w_glu, b_glu, out_g) = s5p
    (conv_w, conv_b, b_i, b_f, norm_g) = mlp
    ds = w_glu.shape[1]
    dml = norm_g.shape[0]
    dh = dml // ML_HEADS
    nmain = ds + 4 * dml
    bcat, ccat, ar, ai, asr, asi = _s5_params(a_re, a_im, log_dt, b_re, b_im, c_re, c_im, seg)
    nst = ar.shape[1]
    nblk = bcat.shape[0]
    r = jnp.arange(tile)
    src = (r % SUBLANES) * seg + r // SUBLANES
    perm = (jnp.arange(tile)[None, :] == src[:, None]).astype(BF16)
    wg = w_in[:, nmain:].T.astype(BF16)
    gb = jnp.concatenate([b_i, b_f]).reshape(2 * ML_HEADS, 1)
    args = (x.reshape(ntiles, tile, dm), g.reshape(1, dm), w_in[:, :nmain].astype(BF16), wg, gb,
            perm, perm.T, bcat, ar, ai, asr, asi, ccat, d.reshape(1, ds),
            w_glu.astype(BF16), b_glu.reshape(1, ds), out_g.reshape(1, ds),
            conv_w, conv_b.reshape(1, 2 * dml), norm_g.reshape(1, dml))
    in_specs = [pl.BlockSpec((1, tile, dm), lambda s: (jnp.minimum(s, ntiles - 1), 0, 0))]
    in_specs += [_const_spec(a.shape) for a in args[1:]]
    y = pl.pallas_call(
        functools.partial(_mix_kernel, seg=seg, tiles_per_seq=tiles_per_seq),
        grid=(ntiles + 1,),
        in_specs=in_specs,
        out_specs=pl.BlockSpec((1, tile, ds + dml), lambda s: (jnp.maximum(s - 1, 0), 0, 0)),
        out_shape=jax.ShapeDtypeStruct((ntiles, tile, ds + dml), BF16),
        scratch_shapes=[
            pltpu.VMEM((tile, dm), BF16),
            pltpu.VMEM((tile, ds), BF16),
            pltpu.VMEM((2, tile, ds), BF16),
            pltpu.VMEM((nblk, tile, 2 * nst // nblk), F32),
            pltpu.VMEM((nblk, tile, 2 * nst // nblk), F32),
            pltpu.VMEM((nblk, tile, 2 * nst // nblk), BF16),
            pltpu.VMEM((tile, ds), F32),
            pltpu.VMEM((tile, ds), BF16),
            pltpu.VMEM((2, nst), F32),
            pltpu.VMEM((nblk, tile + SUBLANES, 2 * dml // nblk), F32),
            pltpu.VMEM((2, tile, dml), BF16),
            pltpu.VMEM((2, tile, dml), BF16),
            pltpu.VMEM((2, tile, dml), BF16),
            pltpu.VMEM((2, tile, dml), F32),
            pltpu.VMEM((tile, dml), F32),
            pltpu.VMEM((2, 3, tile // ML_CHUNK, 2 * ML_HEADS, ML_CHUNK), F32),
            pltpu.VMEM((ML_HEADS, dh, 2 * dh), F32),
            pltpu.VMEM((2 * ML_HEADS, LANES), F32),
        ],
        compiler_params=pltpu.CompilerParams(
            dimension_semantics=("arbitrary",), vmem_limit_bytes=VMEM_LIMIT),
        name="mixer",
    )(*args)
    return y.reshape(bsz, seq, ds + dml)


def _ffn_kernel(x_ref, y_ref, wo_ref, g_ref, w1_ref, w2_ref, fg_ref, o_ref, *, final):
    x1 = x_ref[...] + jnp.dot(y_ref[...], wo_ref[...], preferred_element_type=F32)
    hn = _rms(x1, g_ref[...]).astype(BF16)
    o_ref[...] = x1
    dff = w1_ref.shape[1]
    for c in range(dff // FFN_COL):
        cs = slice(c * FFN_COL, (c + 1) * FFN_COL)
        a = jnp.maximum(jnp.dot(hn, w1_ref[:, cs], preferred_element_type=F32), 0.0)
        o_ref[...] += jnp.dot((a * a).astype(BF16), w2_ref[cs, :], preferred_element_type=F32)
    if final:
        o_ref[...] = _rms(o_ref[...], fg_ref[...])


def _ffn_block(x2, y2, w_out, g, w1, w2, fg, final):
    n, dm = x2.shape
    tile = FFN_TILE
    args = (x2, y2, w_out.astype(BF16), g.reshape(1, dm), w1.astype(BF16), w2.astype(BF16),
            fg.reshape(1, dm))
    row = lambda w: pl.BlockSpec((tile, w), lambda i: (i, 0))
    in_specs = [row(dm), row(y2.shape[1])] + [_const_spec(a.shape) for a in args[2:]]
    return pl.pallas_call(
        functools.partial(_ffn_kernel, final=final),
        grid=(n // tile,),
        in_specs=in_specs,
        out_specs=row(dm),
        out_shape=jax.ShapeDtypeStruct((n, dm), F32),
        compiler_params=pltpu.CompilerParams(
            dimension_semantics=("arbitrary",), vmem_limit_bytes=VMEM_LIMIT),
        name="out_ffn",
    )(*args)


def kernel(x, norm_mix_g, w_in, s5_a_re, s5_a_im, s5_log_dt, s5_b_re, s5_b_im, s5_c_re, s5_c_im, s5_d, s5_w_glu, s5_b_glu, s5_out_g, ml_conv_w, ml_conv_b, ml_b_i, ml_b_f, ml_norm_g, w_out, norm_ffn_g, w_ff1, w_ff2, final_norm_g):
    bsz, seq, dm = x.shape
    depth = w_in.shape[0]
    for l in range(depth):
        s5p = (s5_a_re[l], s5_a_im[l], s5_log_dt[l], s5_b_re[l], s5_b_im[l], s5_c_re[l],
               s5_c_im[l], s5_d[l], s5_w_glu[l], s5_b_glu[l], s5_out_g[l])
        mlp = (ml_conv_w[l], ml_conv_b[l], ml_b_i[l], ml_b_f[l], ml_norm_g[l])
        y = _mixer(x, norm_mix_g[l], w_in[l], s5p, mlp)
        x = _ffn_block(x.reshape(bsz * seq, dm), y.reshape(bsz * seq, y.shape[2]), w_out[l],
                       norm_ffn_g[l], w_ff1[l], w_ff2[l], final_norm_g,
                       final=(l == depth - 1)).reshape(bsz, seq, dm)
    return x
```

```python
import functools
import math

import jax
import jax.numpy as jnp
from jax import lax
from jax.experimental import pallas as pl
from jax.experimental.pallas import tpu as pltpu

EPS = 1e-6
SSM_GROUP = 16
SSM_STATE = 64
ML_HEADS = 4
CONV_WIDTH = 4

SUBLANES = 8
LANES = 128
V7X_VMEM_BYTES = 64 * 1024 * 1024
VMEM_LIMIT = V7X_VMEM_BYTES - 8 * 1024 * 1024

MIX_TILE = 512
MIX_ROW_BLOCK = 128
ML_CHUNK = LANES
FFN_TILE = 512
FFN_COL = 1024
S5_BLOCK_GROUPS = 8
S5_SCAN_PIECE = 16
S5_SUBLANE_STATES = 256

F32 = jnp.float32
BF16 = jnp.bfloat16


def _rms(x, g):
    ms = jnp.mean(x * x, axis=-1, keepdims=True)
    return x * lax.rsqrt(ms + EPS) * g


def _sigmoid(x):
    return 1.0 / (1.0 + jnp.exp(-x))


def _log_sigmoid(x):
    return jnp.minimum(x, 0.0) - jnp.log(1.0 + jnp.exp(-jnp.abs(x)))


def _gelu_tanh(x):
    c = math.sqrt(2.0 / math.pi)
    return x * (0.5 * (1.0 + jnp.tanh(c * (x + 0.044715 * (x * x * x)))))


def _const_spec(shape):
    nd = len(shape)
    return pl.BlockSpec(shape, lambda *_: (0,) * nd, pipeline_mode=pl.Buffered(1))


def _run_interleaved(streams):
    totals = [sum(p[1] for p in s) for s in streams]
    pos = [0] * len(streams)
    spent = [0.0] * len(streams)
    done = set()
    while any(pos[i] < len(s) for i, s in enumerate(streams)):
        ready = [i for i, s in enumerate(streams)
                 if pos[i] < len(s) and all(r in done for r in s[pos[i]][2])]
        assert ready, "interleaving deadlock"
        i = min(ready, key=lambda k: spent[k] / totals[k])
        name, cost, _, fn = streams[i][pos[i]]
        fn()
        done.add(name)
        spent[i] += cost
        pos[i] += 1


def _s5_bu(j, r0, nrow, u_ref, bcat_ref, dn_ref):
    gw = u_ref.shape[1] // bcat_ref.shape[0]
    hw = bcat_ref.shape[2] // 2
    v = jnp.dot(u_ref[r0:r0 + nrow, j * gw:(j + 1) * gw], bcat_ref[j],
                preferred_element_type=F32)
    per_sub = hw // S5_SUBLANE_STATES
    for part in range(2):
        for o in range(per_sub):
            for half in range(S5_SUBLANE_STATES // LANES):
                lo = part * hw + o * S5_SUBLANE_STATES + half * LANES
                dn_ref[2 * part + half,
                       pl.ds(SUBLANES * r0 + j * per_sub + o, nrow, stride=SUBLANES), :] = (
                    v[:, lo:lo + LANES])


def _s5_scan_pieces(ad_ref, dn_ref, xd_ref, st_ref, tile):
    st = {}

    def piece(t0):
        def fn():
            a = [ad_ref[i] for i in range(4)]
            s = st.get("s")
            if s is None:
                s = [st_ref[i] for i in range(4)]
            for t in range(t0, t0 + S5_SCAN_PIECE):
                rows = slice(t * SUBLANES, (t + 1) * SUBLANES)
                d = [dn_ref[i, rows, :] for i in range(4)]
                s = [a[0] * s[0] - a[2] * s[2] + d[0], a[1] * s[1] - a[3] * s[3] + d[1],
                     a[0] * s[2] + a[2] * s[0] + d[2], a[1] * s[3] + a[3] * s[1] + d[3]]
                for i in range(4):
                    xd_ref[i, rows, :] = s[i]
            st["s"] = s
            if t0 + S5_SCAN_PIECE == tile:
                for i in range(4):
                    st_ref[i] = s[i]
        return fn

    pieces = []
    prev = []
    for q in range(tile // S5_SCAN_PIECE):
        pieces.append((f"scan{q}", 110, prev, piece(q * S5_SCAN_PIECE)))
        prev = [f"scan{q}"]
    return pieces


def _s5_cproj(j, r0, nrow, xd_ref, ccat_ref, ys_ref):
    gw = ccat_ref.shape[2]
    hw = ccat_ref.shape[1] // 2
    per_sub = hw // S5_SUBLANE_STATES
    cols = []
    for part in range(2):
        for o in range(per_sub):
            for half in range(S5_SUBLANE_STATES // LANES):
                cols.append(xd_ref[2 * part + half,
                                   pl.ds(SUBLANES * r0 + j * per_sub + o, nrow, stride=SUBLANES),
                                   :])
    lhs = jnp.concatenate(cols, axis=1).astype(BF16)
    ys_ref[r0:r0 + nrow, j * gw:(j + 1) * gw] = jnp.dot(lhs, ccat_ref[j],
                                                        preferred_element_type=F32)


def _s5_back(rows, ys_ref, u_ref, d_ref, wglu_ref, bglu_ref, og_ref):
    y = ys_ref[rows, :] + d_ref[...] * u_ref[rows, :].astype(F32)
    z = _gelu_tanh(y)
    gate = _sigmoid(jnp.dot(z.astype(BF16), wglu_ref[...], preferred_element_type=F32)
                    + bglu_ref[...])
    return _rms(z * gate, og_ref[...]).astype(BF16)


def _s5_params(a_re, a_im, log_dt, b_re, b_im, c_re, c_im):
    dt = jnp.exp(log_dt)[:, None]
    mag = jnp.exp(a_re * dt)
    ab_re = mag * jnp.cos(a_im * dt)
    ab_im = mag * jnp.sin(a_im * dt)
    den = jnp.square(a_re) + jnp.square(a_im)
    zr = ab_re - 1.0
    s_re = (zr * a_re + ab_im * a_im) / den
    s_im = (ab_im * a_re - zr * a_im) / den
    bb_re = s_re[..., None] * b_re - s_im[..., None] * b_im
    bb_im = s_re[..., None] * b_im + s_im[..., None] * b_re

    g, n, p = bb_re.shape
    gb = S5_BLOCK_GROUPS
    nblk = g // gb
    eye = jnp.eye(gb, dtype=F32)

    def in_blocks(bb):
        t = bb.reshape(nblk, gb, n, p)
        m = jnp.einsum('jgnp,gh->jgphn', t, eye)
        return m.reshape(nblk, gb * p, gb * n)

    def out_blocks(cc):
        t = cc.reshape(nblk, gb, p, n)
        m = jnp.einsum('jgpn,gh->jgnhp', t, eye)
        return m.reshape(nblk, gb * n, gb * p)

    bcat = jnp.concatenate([in_blocks(bb_re), in_blocks(bb_im)], axis=2).astype(BF16)
    ccat = jnp.concatenate([out_blocks(c_re), -out_blocks(c_im)], axis=1).astype(BF16)

    def dense(v):
        t = v.reshape(SUBLANES, S5_SUBLANE_STATES // LANES, LANES)
        return [t[:, h, :] for h in range(S5_SUBLANE_STATES // LANES)]

    adense = jnp.stack(dense(ab_re) + dense(ab_im))
    return bcat, ccat, adense


def _ml_conv(c, part, nparts, ncol, cw_ref, cb_ref, xc_ref, q_ref, k_ref):
    tile, dml = q_ref.shape
    dh = dml // ML_HEADS
    tail = SUBLANES
    nrow = tile // nparts
    r0 = part * nrow
    cols = slice(c * ncol, (c + 1) * ncol)
    acc = cb_ref[:, cols]
    for j in range(CONV_WIDTH):
        off = r0 + tail - (CONV_WIDTH - 1) + j
        acc = acc + cw_ref[j:j + 1, cols] * xc_ref[c, off:off + nrow, :]
    if part == nparts - 1:
        xc_ref[c, 0:tail, :] = xc_ref[c, tile:tile + tail, :]
    qk = acc * _sigmoid(acc)
    rows = slice(r0, r0 + nrow)
    if (c + 1) * ncol <= dml:
        q_ref[rows, cols] = qk.astype(BF16)
    else:
        k_ref[rows, c * ncol - dml:(c + 1) * ncol - dml] = (
            qk * (1.0 / math.sqrt(dh))).astype(BF16)


def _ml_gate_rows(gt, a_ref):
    tile = gt.shape[1]
    chunk = ML_CHUNK
    lf = _log_sigmoid(gt)
    lane = lax.broadcasted_iota(jnp.int32, gt.shape, 1) % chunk
    neg_inf = jnp.float32(-jnp.inf)
    bc = lf
    shift = 1
    while shift < chunk:
        bc = bc + jnp.where(lane >= shift, pltpu.roll(bc, shift, axis=1), 0.0)
        shift *= 2
    a8 = gt - pltpu.roll(bc, ML_HEADS, axis=0)
    cm = a8
    shift = 1
    while shift < chunk:
        cm = jnp.maximum(cm, jnp.where(lane >= shift, pltpu.roll(cm, shift, axis=1), neg_inf))
        shift *= 2
    for c in range(tile // chunk):
        cs = slice(c * chunk, (c + 1) * chunk)
        a_ref[0, c] = a8[:, cs]
        a_ref[1, c] = cm[:, cs]
        a_ref[2, c] = bc[:, cs]


def _ml_chunk(c, q_ref, k_ref, v_ref, a_ref, h_ref, caug_ref, m_ref):
    tile, dml = q_ref.shape
    dh = dml // ML_HEADS
    chunk = ML_CHUNK
    neg_inf = jnp.float32(-jnp.inf)
    row_i = lax.broadcasted_iota(jnp.int32, (chunk, chunk), 0)
    col_i = lax.broadcasted_iota(jnp.int32, (chunk, chunk), 1)
    tril = col_i <= row_i
    ones_blk = jnp.ones((chunk, dh), BF16)

    def col_bcast(row):
        return jnp.broadcast_to(row, (chunk, chunk)).T

    rows = slice(c * chunk, (c + 1) * chunk)
    a_t, cm_t, bc_t = a_ref[0, c], a_ref[1, c], a_ref[2, c]
    for hd in range(ML_HEADS):
        sl = slice(hd * dh, (hd + 1) * dh)
        qh = q_ref[rows, sl]
        kh = k_ref[rows, sl]
        vh = v_ref[rows, sl]
        a_r = a_t[hd:hd + 1, :]
        cm_r = cm_t[hd:hd + 1, :]
        bc_r = bc_t[ML_HEADS + hd:ML_HEADS + hd + 1, :]
        m_prev = m_ref[hd:hd + 1, :]
        m_col = jnp.maximum(col_bcast(cm_r), m_prev)
        d_in = jnp.exp(jnp.where(tril, a_r - m_col, neg_inf))
        d_st = jnp.exp(m_prev - m_col)
        e = jnp.exp(-(col_bcast(bc_r) + m_col))
        s = lax.dot_general(qh, kh, (((1,), (1,)), ((), ())),
                            preferred_element_type=F32) * d_in
        lhs = jnp.concatenate([s, qh.astype(F32) * d_st], axis=1).astype(BF16)
        vaug = jnp.concatenate([vh, ones_blk], axis=1)
        caug = caug_ref[hd]
        rhs = jnp.concatenate([vaug, caug.astype(BF16)], axis=0)
        na = jnp.dot(lhs, rhs, preferred_element_type=F32)
        h_ref[rows, sl] = na[:, :dh] / jnp.maximum(jnp.abs(na[:, dh:]), e)
        wk_r = d_in[chunk - 1:chunk, :]
        decay = d_st[chunk - 1:chunk, :]
        kwt = (kh.T.astype(F32) * wk_r).astype(BF16)
        upd = jnp.dot(kwt, vaug, preferred_element_type=F32)
        caug_ref[hd] = jnp.concatenate([decay, decay], axis=1) * caug + upd
        m_ref[hd:hd + 1, :] = (bc_r[:, chunk - 1:chunk]
                               + jnp.maximum(cm_r[:, chunk - 1:chunk], m_prev))


def _ml_back(hd, o_ref, h_ref, ng_ref):
    dh = o_ref.shape[1] // ML_HEADS
    sl = slice(hd * dh, (hd + 1) * dh)
    hv = _sigmoid(o_ref[:, sl]) * h_ref[:, sl]
    return _rms(hv, ng_ref[:, sl]).astype(BF16)


def _mix_kernel(x_ref, g_ref, win_ref, wg_ref, gb_ref,
                bcat_ref, ad_ref, ccat_ref, d_ref, wglu_ref, bglu_ref, og_ref,
                cw_ref, cb_ref, ng_ref,
                y_ref,
                hb_ref, u2_ref, dn_ref, xd_ref, ys_ref, st_ref,
                xc_ref, q2_ref, k2_ref, v2_ref, o2_ref, h_ref, a2_ref, caug_ref, m_ref,
                *, tiles_per_seq):
    tile = hb_ref.shape[0]
    ds = u2_ref.shape[2]
    dml = q2_ref.shape[2]
    nblk = bcat_ref.shape[0]
    ncol = 2 * dml // nblk
    tail = SUBLANES
    step = pl.program_id(0)
    front = step % 2
    back = 1 - front

    @pl.when(step == 0)
    def _():
        u2_ref[...] = jnp.zeros_like(u2_ref)
        dn_ref[...] = jnp.zeros_like(dn_ref)
        q2_ref[...] = jnp.zeros_like(q2_ref)
        k2_ref[...] = jnp.zeros_like(k2_ref)
        v2_ref[...] = jnp.zeros_like(v2_ref)
        o2_ref[...] = jnp.zeros_like(o2_ref)
        a2_ref[...] = jnp.zeros_like(a2_ref)
        st_ref[...] = jnp.zeros_like(st_ref)
        caug_ref[...] = jnp.zeros_like(caug_ref)
        m_ref[...] = jnp.zeros_like(m_ref)

    @pl.when(step % tiles_per_seq == 0)
    def _():
        xc_ref[:, 0:tail, :] = jnp.zeros((nblk, tail, ncol), F32)

    @pl.when(step % tiles_per_seq == 1 % tiles_per_seq)
    def _():
        st_ref[...] = jnp.zeros_like(st_ref)
        caug_ref[...] = jnp.zeros_like(caug_ref)
        m_ref[...] = jnp.zeros_like(m_ref)

    u_f, q_f, k_f, v_f, o_f, a_f = (r.at[front] for r in
                                    (u2_ref, q2_ref, k2_ref, v2_ref, o2_ref, a2_ref))
    u_b, q_b, k_b, v_b, o_b, a_b = (r.at[back] for r in
                                    (u2_ref, q2_ref, k2_ref, v2_ref, o2_ref, a2_ref))

    rb = MIX_ROW_BLOCK
    nrb = tile // rb
    rblocks = [slice(r * rb, (r + 1) * rb) for r in range(nrb)]

    def proj(rows, lo, hi):
        return jnp.dot(hb_ref[rows, :], win_ref[:, lo:hi], preferred_element_type=F32)

    def f_rms(rows):
        def fn():
            hb_ref[rows, :] = _rms(x_ref[0, rows, :], g_ref[...]).astype(BF16)
        return fn

    def f_u(rows):
        def fn():
            u_f[rows, :] = proj(rows, 0, ds).astype(BF16)
        return fn

    def f_gates():
        gt = lax.dot_general(wg_ref[...], hb_ref[...], (((1,), (1,)), ((), ())),
                             preferred_element_type=F32) + gb_ref[...]
        _ml_gate_rows(gt, a_f)

    def f_qk(c, r):
        def fn():
            xc_ref[c, tail + r * rb:tail + (r + 1) * rb, :] = proj(
                rblocks[r], ds + c * ncol, ds + (c + 1) * ncol)
        return fn

    def f_v(rows):
        def fn():
            v_f[rows, :] = proj(rows, ds + 2 * dml, ds + 3 * dml).astype(BF16)
        return fn

    def f_o(rows):
        def fn():
            o_f[rows, :] = proj(rows, ds + 3 * dml, ds + 4 * dml)
        return fn

    def b_s5(rows):
        def fn():
            y_ref[0, rows, :ds] = _s5_back(rows, ys_ref, u_b, d_ref, wglu_ref, bglu_ref, og_ref)
        return fn

    def b_ml(hd):
        def fn():
            dh = dml // ML_HEADS
            y_ref[0, :, ds + hd * dh:ds + (hd + 1) * dh] = _ml_back(hd, o_b, h_ref, ng_ref)
        return fn

    nchunk = tile // ML_CHUNK
    scan_pieces = _s5_scan_pieces(ad_ref, dn_ref, xd_ref, st_ref, tile)
    steps_per_rb = rb // S5_SCAN_PIECE
    all_rms = [f"rms{r}" for r in range(nrb)]
    mxu_stream = [(f"u{r}", 128, [f"rms{r}"], f_u(rblocks[r])) for r in range(nrb)]
    mxu_stream += [(f"qk0_{r}", 128, [f"rms{r}"], f_qk(0, r)) for r in range(nrb)]
    mxu_stream += [(f"qk1_{r}", 128, [f"rms{r}"], f_qk(1, r)) for r in range(nrb)]
    mxu_stream += [(f"v{r}", 128, [f"rms{r}"], f_v(rblocks[r])) for r in range(nrb)]
    mxu_stream += [(f"qk2_{r}", 128, [f"rms{r}"], f_qk(2, r)) for r in range(nrb)]
    mxu_stream += [(f"o{r}", 256, [f"rms{r}"], f_o(rblocks[r])) for r in range(nrb)]
    mxu_stream += [(f"qk3_{r}", 128, [f"rms{r}"], f_qk(3, r)) for r in range(nrb)]
    for r in range(nrb):
        passed = f"scan{(r + 1) * steps_per_rb - 1}"
        mxu_stream += [(f"bu{c}_{r}", 150, [f"u{r}", passed], functools.partial(
            _s5_bu, c, r * rb, rb, u_f, bcat_ref, dn_ref)) for c in range(nblk)]

    vpu_stream = [(f"rms{r}", 150, [], f_rms(rblocks[r])) for r in range(nrb)]
    vpu_stream.append(("gates", 300, all_rms, f_gates))
    conv_order = [(j, part) for j in range(nblk) for part in range(nrb)]
    per_conv = len(scan_pieces) // len(conv_order)
    for i, (j, part) in enumerate(conv_order):
        vpu_stream.extend(scan_pieces[i * per_conv:(i + 1) * per_conv])
        vpu_stream.append((f"conv{j}_{part}", 275, [f"qk{j}_{part}"], functools.partial(
            _ml_conv, j, part, nrb, ncol, cw_ref, cb_ref, xc_ref, q_f, k_f)))
    vpu_stream.extend(scan_pieces[len(conv_order) * per_conv:])
    _run_interleaved([mxu_stream, vpu_stream])

    chunk_stream = [(f"chunk{c}", 1500, [], functools.partial(
        _ml_chunk, c, q_b, k_b, v_b, a_b, h_ref, caug_ref, m_ref)) for c in range(nchunk)]
    chunk_stream += [(f"mlback{hd}", 300, [], b_ml(hd)) for hd in range(ML_HEADS)]
    out_stream = []
    for r in range(nrb):
        out_stream += [(f"cproj{c}_{r}", 200, [], functools.partial(
            _s5_cproj, c, r * rb, rb, xd_ref, ccat_ref, ys_ref)) for c in range(nblk)]
        out_stream.append((f"s5back{r}", 350, [], b_s5(rblocks[r])))
    _run_interleaved([chunk_stream, out_stream])


def _mixer(x, g, w_in, s5p, mlp):
    bsz, seq, dm = x.shape
    tile = MIX_TILE
    tiles_per_seq = seq // tile
    ntiles = bsz * tiles_per_seq
    (a_re, a_im, log_dt, b_re, b_im, c_re, c_im, d, w_glu, b_glu, out_g) = s5p
    (conv_w, conv_b, b_i, b_f, norm_g) = mlp
    ds = w_glu.shape[1]
    dml = norm_g.shape[0]
    dh = dml // ML_HEADS
    nmain = ds + 4 * dml
    bcat, ccat, adense = _s5_params(a_re, a_im, log_dt, b_re, b_im, c_re, c_im)
    nblk = bcat.shape[0]
    wg = w_in[:, nmain:].T.astype(BF16)
    gb = jnp.concatenate([b_i, b_f]).reshape(2 * ML_HEADS, 1)
    args = (x.reshape(ntiles, tile, dm), g.reshape(1, dm), w_in[:, :nmain].astype(BF16), wg, gb,
            bcat, adense, ccat, d.reshape(1, ds),
            w_glu.astype(BF16), b_glu.reshape(1, ds), out_g.reshape(1, ds),
            conv_w, conv_b.reshape(1, 2 * dml), norm_g.reshape(1, dml))
    in_specs = [pl.BlockSpec((1, tile, dm), lambda s: (jnp.minimum(s, ntiles - 1), 0, 0))]
    in_specs += [_const_spec(a.shape) for a in args[1:]]
    nslab = adense.shape[0]
    y = pl.pallas_call(
        functools.partial(_mix_kernel, tiles_per_seq=tiles_per_seq),
        grid=(ntiles + 1,),
        in_specs=in_specs,
        out_specs=pl.BlockSpec((1, tile, ds + dml), lambda s: (jnp.maximum(s - 1, 0), 0, 0)),
        out_shape=jax.ShapeDtypeStruct((ntiles, tile, ds + dml), BF16),
        scratch_shapes=[
            pltpu.VMEM((tile, dm), BF16),
            pltpu.VMEM((2, tile, ds), BF16),
            pltpu.VMEM((nslab, SUBLANES * tile, LANES), F32),
            pltpu.VMEM((nslab, SUBLANES * tile, LANES), F32),
            pltpu.VMEM((tile, ds), F32),
            pltpu.VMEM((nslab, SUBLANES, LANES), F32),
            pltpu.VMEM((nblk, tile + SUBLANES, 2 * dml // nblk), F32),
            pltpu.VMEM((2, tile, dml), BF16),
            pltpu.VMEM((2, tile, dml), BF16),
            pltpu.VMEM((2, tile, dml), BF16),
            pltpu.VMEM((2, tile, dml), F32),
            pltpu.VMEM((tile, dml), F32),
            pltpu.VMEM((2, 3, tile // ML_CHUNK, 2 * ML_HEADS, ML_CHUNK), F32),
            pltpu.VMEM((ML_HEADS, dh, 2 * dh), F32),
            pltpu.VMEM((2 * ML_HEADS, LANES), F32),
        ],
        compiler_params=pltpu.CompilerParams(
            dimension_semantics=("arbitrary",), vmem_limit_bytes=VMEM_LIMIT),
        name="mixer",
    )(*args)
    return y.reshape(bsz, seq, ds + dml)


def _ffn_kernel(x_ref, y_ref, wo_ref, g_ref, w1_ref, w2_ref, fg_ref, o_ref, *, final):
    x1 = x_ref[...] + jnp.dot(y_ref[...], wo_ref[...], preferred_element_type=F32)
    hn = _rms(x1, g_ref[...]).astype(BF16)
    o_ref[...] = x1
    dff = w1_ref.shape[1]
    for c in range(dff // FFN_COL):
        cs = slice(c * FFN_COL, (c + 1) * FFN_COL)
        a = jnp.maximum(jnp.dot(hn, w1_ref[:, cs], preferred_element_type=F32), 0.0)
        o_ref[...] += jnp.dot((a * a).astype(BF16), w2_ref[cs, :], preferred_element_type=F32)
    if final:
        o_ref[...] = _rms(o_ref[...], fg_ref[...])


def _ffn_block(x2, y2, w_out, g, w1, w2, fg, final):
    n, dm = x2.shape
    tile = FFN_TILE
    args = (x2, y2, w_out.astype(BF16), g.reshape(1, dm), w1.astype(BF16), w2.astype(BF16),
            fg.reshape(1, dm))
    row = lambda w: pl.BlockSpec((tile, w), lambda i: (i, 0))
    in_specs = [row(dm), row(y2.shape[1])] + [_const_spec(a.shape) for a in args[2:]]
    return pl.pallas_call(
        functools.partial(_ffn_kernel, final=final),
        grid=(n // tile,),
        in_specs=in_specs,
        out_specs=row(dm),
        out_shape=jax.ShapeDtypeStruct((n, dm), F32),
        compiler_params=pltpu.CompilerParams(
            dimension_semantics=("arbitrary",), vmem_limit_bytes=VMEM_LIMIT),
        name="out_ffn",
    )(*args)


def kernel(x, norm_mix_g, w_in, s5_a_re, s5_a_im, s5_log_dt, s5_b_re, s5_b_im, s5_c_re, s5_c_im, s5_d, s5_w_glu, s5_b_glu, s5_out_g, ml_conv_w, ml_conv_b, ml_b_i, ml_b_f, ml_norm_g, w_out, norm_ffn_g, w_ff1, w_ff2, final_norm_g):
    bsz, seq, dm = x.shape
    depth = w_in.shape[0]
    for l in range(depth):
        s5p = (s5_a_re[l], s5_a_im[l], s5_log_dt[l], s5_b_re[l], s5_b_im[l], s5_c_re[l],
               s5_c_im[l], s5_d[l], s5_w_glu[l], s5_b_glu[l], s5_out_g[l])
        mlp = (ml_conv_w[l], ml_conv_b[l], ml_b_i[l], ml_b_f[l], ml_norm_g[l])
        y = _mixer(x, norm_mix_g[l], w_in[l], s5p, mlp)
        x = _ffn_block(x.reshape(bsz * seq, dm), y.reshape(bsz * seq, y.shape[2]), w_out[l],
                       norm_ffn_g[l], w_ff1[l], w_ff2[l], final_norm_g,
                       final=(l == depth - 1)).reshape(bsz, seq, dm)
    return x
```

```python
import functools
import math

import jax
import jax.numpy as jnp
from jax import lax
from jax.experimental import pallas as pl
from jax.experimental.pallas import tpu as pltpu

EPS = 1e-6
SSM_GROUP = 16
SSM_STATE = 64
ML_HEADS = 4
CONV_WIDTH = 4

SUBLANES = 8
LANES = 128
V7X_VMEM_BYTES = 64 * 1024 * 1024
VMEM_LIMIT = V7X_VMEM_BYTES - 8 * 1024 * 1024

MIX_TILE = 512
MIX_ROW_BLOCK = 128
ML_CHUNK = LANES
FFN_TILE = 512
FFN_COL = 1024
S5_BLOCK_GROUPS = 8
S5_SCAN_PIECE = 16

F32 = jnp.float32
BF16 = jnp.bfloat16


def _rms(x, g):
    ms = jnp.mean(x * x, axis=-1, keepdims=True)
    return x * lax.rsqrt(ms + EPS) * g


def _sigmoid(x):
    return 1.0 / (1.0 + jnp.exp(-x))


def _log_sigmoid(x):
    return jnp.minimum(x, 0.0) - jnp.log(1.0 + jnp.exp(-jnp.abs(x)))


def _gelu_tanh(x):
    c = math.sqrt(2.0 / math.pi)
    return x * (0.5 * (1.0 + jnp.tanh(c * (x + 0.044715 * (x * x * x)))))


def _const_spec(shape):
    nd = len(shape)
    return pl.BlockSpec(shape, lambda *_: (0,) * nd, pipeline_mode=pl.Buffered(1))


def _run_interleaved(streams):
    totals = [sum(p[1] for p in s) for s in streams]
    pos = [0] * len(streams)
    spent = [0.0] * len(streams)
    done = set()
    while any(pos[i] < len(s) for i, s in enumerate(streams)):
        ready = [i for i, s in enumerate(streams)
                 if pos[i] < len(s) and all(r in done for r in s[pos[i]][2])]
        assert ready, "interleaving deadlock"
        i = min(ready, key=lambda k: spent[k] / totals[k])
        name, cost, _, fn = streams[i][pos[i]]
        fn()
        done.add(name)
        spent[i] += cost
        pos[i] += 1


def _s5_bu(j, rows, up_ref, bcat_ref, bu_ref):
    hw2 = bcat_ref.shape[2]
    gw = up_ref.shape[1] // bcat_ref.shape[0]
    bu_ref[j, rows, :] = jnp.dot(
        up_ref[rows, j * gw:(j + 1) * gw], bcat_ref[j], preferred_element_type=F32)


def _s5_scan_pieces(j, ar_ref, ai_ref, asr_ref, asi_ref, bu_ref, xs_ref, xb_ref, carry_ref, seg):
    nst = ar_ref.shape[1]
    hw = nst // (nst // (S5_BLOCK_GROUPS * SSM_STATE))
    cr = slice(0, hw)
    ci = slice(hw, 2 * hw)
    blk = slice(j * hw, (j + 1) * hw)
    pair = 2 * SUBLANES
    st = {}

    def coeffs():
        return (jnp.broadcast_to(ar_ref[:, blk], (SUBLANES, hw)),
                jnp.broadcast_to(ai_ref[:, blk], (SUBLANES, hw)))

    def pass1(s0):
        def fn():
            ar, ai = coeffs()
            sr, si = st.get("s", (jnp.zeros((SUBLANES, hw), F32),) * 2)
            for s in range(s0, s0 + S5_SCAN_PIECE):
                rows = slice(s * SUBLANES, (s + 1) * SUBLANES)
                sr, si = (ar * sr - ai * si + bu_ref[j, rows, cr],
                          ar * si + ai * sr + bu_ref[j, rows, ci])
                xs_ref[j, rows, cr] = sr
                xs_ref[j, rows, ci] = si
            st["s"] = (sr, si)
        return fn

    def carry():
        fr, fi = st["s"]
        asr = asr_ref[:, blk]
        asi = asi_ref[:, blk]
        rows_r = [carry_ref[0:1, blk]]
        rows_i = [carry_ref[1:2, blk]]
        for k in range(SUBLANES):
            pr, pi = rows_r[-1], rows_i[-1]
            rows_r.append(asr * pr - asi * pi + fr[k:k + 1])
            rows_i.append(asr * pi + asi * pr + fi[k:k + 1])
        carry_ref[0:1, blk] = rows_r[SUBLANES]
        carry_ref[1:2, blk] = rows_i[SUBLANES]
        st["w"] = (jnp.concatenate(rows_r[:SUBLANES], axis=0),
                   jnp.concatenate(rows_i[:SUBLANES], axis=0))

    def pass2(s0):
        def fn():
            ar, ai = coeffs()
            wr, wi = st["w"]
            for s in range(s0, s0 + S5_SCAN_PIECE, 2):
                r0 = slice(s * SUBLANES, (s + 1) * SUBLANES)
                r1 = slice((s + 1) * SUBLANES, (s + 2) * SUBLANES)
                w1r, w1i = ar * wr - ai * wi, ar * wi + ai * wr
                wr, wi = ar * w1r - ai * w1i, ar * w1i + ai * w1r
                both = slice(s * SUBLANES, (s + 2) * SUBLANES)
                xb_ref[j, both, cr] = jnp.concatenate(
                    [xs_ref[j, r0, cr] + w1r, xs_ref[j, r1, cr] + wr], axis=0).astype(BF16)
                xb_ref[j, both, ci] = jnp.concatenate(
                    [xs_ref[j, r0, ci] + w1i, xs_ref[j, r1, ci] + wi], axis=0).astype(BF16)
            st["w"] = (wr, wi)
        return fn

    pieces = []
    prev = []
    n = seg // S5_SCAN_PIECE
    for q in range(n):
        name = f"p1_{j}_{q}"
        pieces.append((name, 150, prev, pass1(q * S5_SCAN_PIECE)))
        prev = [name]
    pieces.append((f"carry{j}", 60, prev, carry))
    prev = [f"carry{j}"]
    for q in range(n):
        name = f"p2_{j}_{q}"
        pieces.append((name, 150, prev, pass2(q * S5_SCAN_PIECE)))
        prev = [name]
    return pieces, f"p1_{j}_{n - 1}", f"p2_{j}_{n - 1}"


def _s5_cproj(j, rows, xb_ref, ccat_ref, ys_ref):
    hw2 = ccat_ref.shape[1]
    gw = ccat_ref.shape[2]
    ys_ref[rows, j * gw:(j + 1) * gw] = jnp.dot(xb_ref[j, rows, :],
                                                ccat_ref[j], preferred_element_type=F32)


def _s5_back(rows, ys_ref, up_ref, d_ref, wglu_ref, bglu_ref, og_ref, zn_ref):
    y = ys_ref[rows, :] + d_ref[...] * up_ref[rows, :].astype(F32)
    z = _gelu_tanh(y)
    gate = _sigmoid(jnp.dot(z.astype(BF16), wglu_ref[...], preferred_element_type=F32)
                    + bglu_ref[...])
    zn_ref[rows, :] = _rms(z * gate, og_ref[...]).astype(BF16)


def _s5_params(a_re, a_im, log_dt, b_re, b_im, c_re, c_im, seg):
    dt = jnp.exp(log_dt)[:, None]
    mag = jnp.exp(a_re * dt)
    ab_re = mag * jnp.cos(a_im * dt)
    ab_im = mag * jnp.sin(a_im * dt)
    den = jnp.square(a_re) + jnp.square(a_im)
    zr = ab_re - 1.0
    s_re = (zr * a_re + ab_im * a_im) / den
    s_im = (ab_im * a_re - zr * a_im) / den
    bb_re = s_re[..., None] * b_re - s_im[..., None] * b_im
    bb_im = s_re[..., None] * b_im + s_im[..., None] * b_re
    mag_s = jnp.exp(a_re * dt * seg)
    as_re = mag_s * jnp.cos(a_im * dt * seg)
    as_im = mag_s * jnp.sin(a_im * dt * seg)

    g, n, p = bb_re.shape
    gb = S5_BLOCK_GROUPS
    nblk = g // gb
    eye = jnp.eye(gb, dtype=F32)

    def in_blocks(bb):
        t = bb.reshape(nblk, gb, n, p)
        m = jnp.einsum('jgnp,gh->jgphn', t, eye)
        return m.reshape(nblk, gb * p, gb * n)

    def out_blocks(cc):
        t = cc.reshape(nblk, gb, p, n)
        m = jnp.einsum('jgpn,gh->jgnhp', t, eye)
        return m.reshape(nblk, gb * n, gb * p)

    bcat = jnp.concatenate([in_blocks(bb_re), in_blocks(bb_im)], axis=2).astype(BF16)
    ccat = jnp.concatenate([out_blocks(c_re), -out_blocks(c_im)], axis=1).astype(BF16)
    flat = lambda v: v.reshape(1, g * n)
    return bcat, ccat, flat(ab_re), flat(ab_im), flat(as_re), flat(as_im)


def _ml_conv(c, part, nparts, ncol, cw_ref, cb_ref, xc_ref, q_ref, k_ref):
    tile, dml = q_ref.shape
    dh = dml // ML_HEADS
    tail = SUBLANES
    nrow = tile // nparts
    r0 = part * nrow
    rows = slice(r0, r0 + nrow)
    for sl in range(ncol // LANES):
        slab = c * (ncol // LANES) + sl
        cols = slice(slab * LANES, (slab + 1) * LANES)
        acc = cb_ref[:, cols]
        for j in range(CONV_WIDTH):
            off = r0 + tail - (CONV_WIDTH - 1) + j
            acc = acc + cw_ref[j:j + 1, cols] * xc_ref[slab, pl.ds(off, nrow, stride=1), :]
        if part == nparts - 1:
            xc_ref[slab, 0:tail, :] = xc_ref[slab, tile:tile + tail, :]
        qk = acc * _sigmoid(acc)
        if cols.stop <= dml:
            q_ref[rows, cols] = qk.astype(BF16)
        else:
            k_ref[rows, cols.start - dml:cols.stop - dml] = (
                qk * (1.0 / math.sqrt(dh))).astype(BF16)


def _ml_gate_rows(gt, a_ref):
    tile = gt.shape[1]
    chunk = ML_CHUNK
    lf = _log_sigmoid(gt)
    lane = lax.broadcasted_iota(jnp.int32, gt.shape, 1) % chunk
    neg_inf = jnp.float32(-jnp.inf)
    bc = lf
    shift = 1
    while shift < chunk:
        bc = bc + jnp.where(lane >= shift, pltpu.roll(bc, shift, axis=1), 0.0)
        shift *= 2
    a8 = gt - pltpu.roll(bc, ML_HEADS, axis=0)
    cm = a8
    shift = 1
    while shift < chunk:
        cm = jnp.maximum(cm, jnp.where(lane >= shift, pltpu.roll(cm, shift, axis=1), neg_inf))
        shift *= 2
    for c in range(tile // chunk):
        cs = slice(c * chunk, (c + 1) * chunk)
        a_ref[0, c] = a8[:, cs]
        a_ref[1, c] = cm[:, cs]
        a_ref[2, c] = bc[:, cs]


def _ml_chunk(c, q_ref, k_ref, v_ref, a_ref, h_ref, caug_ref, m_ref):
    tile, dml = q_ref.shape
    dh = dml // ML_HEADS
    chunk = ML_CHUNK
    neg_inf = jnp.float32(-jnp.inf)
    row_i = lax.broadcasted_iota(jnp.int32, (chunk, chunk), 0)
    col_i = lax.broadcasted_iota(jnp.int32, (chunk, chunk), 1)
    tril = col_i <= row_i
    ones_blk = jnp.ones((chunk, dh), BF16)

    def col_bcast(row):
        return jnp.broadcast_to(row, (chunk, chunk)).T

    rows = slice(c * chunk, (c + 1) * chunk)
    a_t, cm_t, bc_t = a_ref[0, c], a_ref[1, c], a_ref[2, c]
    for hd in range(ML_HEADS):
        sl = slice(hd * dh, (hd + 1) * dh)
        qh = q_ref[rows, sl]
        kh = k_ref[rows, sl]
        vh = v_ref[rows, sl]
        a_r = a_t[hd:hd + 1, :]
        cm_r = cm_t[hd:hd + 1, :]
        bc_r = bc_t[ML_HEADS + hd:ML_HEADS + hd + 1, :]
        m_prev = m_ref[hd:hd + 1, :]
        m_col = jnp.maximum(col_bcast(cm_r), m_prev)
        d_in = jnp.exp(jnp.where(tril, a_r - m_col, neg_inf))
        d_st = jnp.exp(m_prev - m_col)
        e = jnp.exp(-(col_bcast(bc_r) + m_col))
        s = lax.dot_general(qh, kh, (((1,), (1,)), ((), ())),
                            preferred_element_type=F32) * d_in
        lhs = jnp.concatenate([s, qh.astype(F32) * d_st], axis=1).astype(BF16)
        vaug = jnp.concatenate([vh, ones_blk], axis=1)
        caug = caug_ref[hd]
        rhs = jnp.concatenate([vaug, caug.astype(BF16)], axis=0)
        na = jnp.dot(lhs, rhs, preferred_element_type=F32)
        h_ref[rows, sl] = na[:, :dh] / jnp.maximum(jnp.abs(na[:, dh:]), e)
        wk_r = d_in[chunk - 1:chunk, :]
        decay = d_st[chunk - 1:chunk, :]
        kwt = (kh.T.astype(F32) * wk_r).astype(BF16)
        upd = jnp.dot(kwt, vaug, preferred_element_type=F32)
        caug_ref[hd] = jnp.concatenate([decay, decay], axis=1) * caug + upd
        m_ref[hd:hd + 1, :] = (bc_r[:, chunk - 1:chunk]
                               + jnp.maximum(cm_r[:, chunk - 1:chunk], m_prev))


def _ml_back(hd, o_ref, h_ref, ng_ref):
    dh = o_ref.shape[1] // ML_HEADS
    sl = slice(hd * dh, (hd + 1) * dh)
    hv = _sigmoid(o_ref[:, sl]) * h_ref[:, sl]
    return _rms(hv, ng_ref[:, sl]).astype(BF16)


def _mix_kernel(x_ref, g_ref, win_ref, wg_ref, gb_ref,
                p_ref, pt_ref, bcat_ref, ar_ref, ai_ref, asr_ref, asi_ref, ccat_ref, d_ref,
                wglu_ref, bglu_ref, og_ref, cw_ref, cb_ref, ng_ref,
                y_ref,
                hb_ref, ub_ref, up2_ref, bu_ref, xs_ref, xb_ref, ys_ref, zn_ref, carry_ref,
                xc_ref, q2_ref, k2_ref, v2_ref, o2_ref, h_ref, a2_ref, caug_ref, m_ref,
                *, seg, tiles_per_seq, front):
    tile = hb_ref.shape[0]
    ds = up2_ref.shape[2]
    dml = q2_ref.shape[2]
    nblk = bcat_ref.shape[0]
    ncol = 2 * dml // nblk
    tail = SUBLANES
    step = pl.program_id(0)
    back = 1 - front

    @pl.when(step == 0)
    def _():
        up2_ref[...] = jnp.zeros_like(up2_ref)
        bu_ref[...] = jnp.zeros_like(bu_ref)
        q2_ref[...] = jnp.zeros_like(q2_ref)
        k2_ref[...] = jnp.zeros_like(k2_ref)
        v2_ref[...] = jnp.zeros_like(v2_ref)
        o2_ref[...] = jnp.zeros_like(o2_ref)
        a2_ref[...] = jnp.zeros_like(a2_ref)
        carry_ref[...] = jnp.zeros_like(carry_ref)
        caug_ref[...] = jnp.zeros_like(caug_ref)
        m_ref[...] = jnp.zeros_like(m_ref)

    @pl.when(step % tiles_per_seq == 0)
    def _():
        xc_ref[:, 0:tail, :] = jnp.zeros((xc_ref.shape[0], tail, LANES), F32)

    @pl.when(step % tiles_per_seq == 1 % tiles_per_seq)
    def _():
        carry_ref[...] = jnp.zeros_like(carry_ref)
        caug_ref[...] = jnp.zeros_like(caug_ref)
        m_ref[...] = jnp.zeros_like(m_ref)

    up_f, q_f, k_f, v_f, o_f, a_f = (r.at[front] for r in
                                     (up2_ref, q2_ref, k2_ref, v2_ref, o2_ref, a2_ref))
    up_b, q_b, k_b, v_b, o_b, a_b = (r.at[back] for r in
                                     (up2_ref, q2_ref, k2_ref, v2_ref, o2_ref, a2_ref))

    rb = MIX_ROW_BLOCK
    nrb = tile // rb
    rblocks = [slice(r * rb, (r + 1) * rb) for r in range(nrb)]

    def proj(rows, lo, hi):
        return jnp.dot(hb_ref[rows, :], win_ref[:, lo:hi], preferred_element_type=F32)

    def f_rms(rows):
        def fn():
            hb_ref[rows, :] = _rms(x_ref[0, rows, :], g_ref[...]).astype(BF16)
        return fn

    def f_u(rows):
        def fn():
            ub_ref[rows, :] = proj(rows, 0, ds).astype(BF16)
        return fn

    def f_perm(rows):
        def fn():
            up_f[rows, :] = jnp.dot(p_ref[rows, :], ub_ref[...],
                                    preferred_element_type=F32).astype(BF16)
        return fn

    def f_gates():
        gt = lax.dot_general(wg_ref[...], hb_ref[...], (((1,), (1,)), ((), ())),
                             preferred_element_type=F32) + gb_ref[...]
        _ml_gate_rows(gt, a_f)

    def f_qk(c, r):
        def fn():
            v = proj(rblocks[r], ds + c * ncol, ds + (c + 1) * ncol)
            for sl in range(ncol // LANES):
                xc_ref[c * (ncol // LANES) + sl, tail + r * rb:tail + (r + 1) * rb, :] = (
                    v[:, sl * LANES:(sl + 1) * LANES])
        return fn

    def f_v(rows):
        def fn():
            v_f[rows, :] = proj(rows, ds + 2 * dml, ds + 3 * dml).astype(BF16)
        return fn

    def f_o(rows):
        def fn():
            o_f[rows, :] = proj(rows, ds + 3 * dml, ds + 4 * dml)
        return fn

    scans = [_s5_scan_pieces(j, ar_ref, ai_ref, asr_ref, asi_ref, bu_ref, xs_ref, xb_ref,
                             carry_ref, seg) for j in range(nblk)]

    def b_unperm(rows):
        def fn():
            y_ref[0, rows, :ds] = jnp.dot(pt_ref[rows, :], zn_ref[...],
                                          preferred_element_type=F32).astype(BF16)
        return fn

    def b_ml(hd):
        def fn():
            dh = dml // ML_HEADS
            y_ref[0, :, ds + hd * dh:ds + (hd + 1) * dh] = _ml_back(hd, o_b, h_ref, ng_ref)
        return fn

    nchunk = tile // ML_CHUNK
    all_rms = [f"rms{r}" for r in range(nrb)]
    mxu_stream = [(f"u{r}", 128, [f"rms{r}"], f_u(rblocks[r])) for r in range(nrb)]
    mxu_stream += [(f"perm{r}", 64, [f"u{q}" for q in range(nrb)], f_perm(rblocks[r]))
                   for r in range(nrb)]
    mxu_stream += [(f"qk0_{r}", 128, [f"rms{r}"], f_qk(0, r)) for r in range(nrb)]
    for c in range(nblk):
        if c + 1 < nblk:
            mxu_stream += [(f"qk{c + 1}_{r}", 128, [f"rms{r}"], f_qk(c + 1, r))
                           for r in range(nrb)]
        mxu_stream += [(f"bu{c}_{r}", 150, [f"perm{r}", scans[c][1]], functools.partial(
            _s5_bu, c, rblocks[r], up_f, bcat_ref, bu_ref)) for r in range(nrb)]
        if c == 0:
            mxu_stream += [(f"v{r}", 128, [f"rms{r}"], f_v(rblocks[r])) for r in range(nrb)]
        if c == 1:
            mxu_stream += [(f"o{r}", 256, [f"rms{r}"], f_o(rblocks[r])) for r in range(nrb)]

    vpu_stream = [(f"rms{r}", 150, [], f_rms(rblocks[r])) for r in range(nrb)]
    vpu_stream.append(("gates", 300, all_rms, f_gates))
    for j in range(nblk):
        vpu_stream.extend(scans[j][0])
        for part in range(nrb):
            vpu_stream.append((f"conv{j}_{part}", 275, [f"qk{j}_{part}"], functools.partial(
                _ml_conv, j, part, nrb, ncol, cw_ref, cb_ref, xc_ref, q_f, k_f)))
    _run_interleaved([mxu_stream, vpu_stream])

    chunk_stream = [(f"chunk{c}", 1500, [], functools.partial(
        _ml_chunk, c, q_b, k_b, v_b, a_b, h_ref, caug_ref, m_ref)) for c in range(nchunk)]
    chunk_stream += [(f"mlback{hd}", 300, [], b_ml(hd)) for hd in range(ML_HEADS)]
    out_stream = [(f"cproj{c}_{r}", 130, [], functools.partial(
        _s5_cproj, c, rblocks[r], xb_ref, ccat_ref, ys_ref))
        for c in range(nblk) for r in range(nrb)]
    out_stream += [(f"s5back{r}", 350, [], functools.partial(
        _s5_back, rblocks[r], ys_ref, up_b, d_ref, wglu_ref, bglu_ref, og_ref, zn_ref))
        for r in range(nrb)]
    out_stream += [(f"unperm{r}", 64, [], b_unperm(rblocks[r])) for r in range(nrb)]
    _run_interleaved([chunk_stream, out_stream])


def _mix_entry(*refs, **static):
    step = pl.program_id(0)
    for parity in range(2):
        pl.when(step % 2 == parity)(functools.partial(_mix_kernel, *refs, front=parity, **static))


def _layer_spec(arr, layer, block=None):
    block = tuple(arr.shape[1:]) if block is None else block
    nd = len(block)
    return pl.BlockSpec((None,) + block, lambda *_: (layer,) + (0,) * nd,
                        pipeline_mode=pl.Buffered(1))


def _mixer(x, layer, p):
    bsz, seq, dm = x.shape
    tile = MIX_TILE
    seg = tile // SUBLANES
    tiles_per_seq = seq // tile
    ntiles = bsz * tiles_per_seq
    ds = p["w_glu"].shape[2]
    dml = p["ml_norm_g"].shape[2]
    dh = dml // ML_HEADS
    nmain = ds + 4 * dml
    nst = p["ar"].shape[2]
    nblk = p["bcat"].shape[1]
    r = jnp.arange(tile)
    src = (r % SUBLANES) * seg + r // SUBLANES
    perm = (jnp.arange(tile)[None, :] == src[:, None]).astype(BF16)
    stacked = [p[k] for k in ("norm_mix_g", "w_in", "wg", "gb")]
    consts = [perm, perm.T]
    stacked2 = [p[k] for k in ("bcat", "ar", "ai", "asr", "asi", "ccat", "s5_d", "w_glu", "b_glu",
                               "s5_out_g", "conv_w", "conv_b", "ml_norm_g")]
    args = [x.reshape(ntiles, tile, dm)] + stacked + consts + stacked2
    in_specs = [pl.BlockSpec((1, tile, dm), lambda s: (jnp.minimum(s, ntiles - 1), 0, 0))]
    in_specs += [_layer_spec(a, layer, (dm, nmain) if a is p["w_in"] else None) for a in stacked]
    in_specs += [_const_spec(a.shape) for a in consts]
    in_specs += [_layer_spec(a, layer) for a in stacked2]
    y = pl.pallas_call(
        functools.partial(_mix_entry, seg=seg, tiles_per_seq=tiles_per_seq),
        grid=(ntiles + 1,),
        in_specs=in_specs,
        out_specs=pl.BlockSpec((1, tile, ds + dml), lambda s: (jnp.maximum(s - 1, 0), 0, 0)),
        out_shape=jax.ShapeDtypeStruct((ntiles, tile, ds + dml), BF16),
        scratch_shapes=[
            pltpu.VMEM((tile, dm), BF16),
            pltpu.VMEM((tile, ds), BF16),
            pltpu.VMEM((2, tile, ds), BF16),
            pltpu.VMEM((nblk, tile, 2 * nst // nblk), F32),
            pltpu.VMEM((nblk, tile, 2 * nst // nblk), F32),
            pltpu.VMEM((nblk, tile, 2 * nst // nblk), BF16),
            pltpu.VMEM((tile, ds), F32),
            pltpu.VMEM((tile, ds), BF16),
            pltpu.VMEM((2, nst), F32),
            pltpu.VMEM((2 * dml // LANES, tile + SUBLANES, LANES), F32),
            pltpu.VMEM((2, tile, dml), BF16),
            pltpu.VMEM((2, tile, dml), BF16),
            pltpu.VMEM((2, tile, dml), BF16),
            pltpu.VMEM((2, tile, dml), F32),
            pltpu.VMEM((tile, dml), F32),
            pltpu.VMEM((2, 3, tile // ML_CHUNK, 2 * ML_HEADS, ML_CHUNK), F32),
            pltpu.VMEM((ML_HEADS, dh, 2 * dh), F32),
            pltpu.VMEM((2 * ML_HEADS, LANES), F32),
        ],
        compiler_params=pltpu.CompilerParams(
            dimension_semantics=("arbitrary",), vmem_limit_bytes=VMEM_LIMIT),
        name="mixer",
    )(*args)
    return y.reshape(bsz, seq, ds + dml)


def _ffn_kernel(x_ref, y_ref, wo_ref, g_ref, w1_ref, w2_ref, fg_ref, o_ref, *, final):
    x1 = x_ref[...] + jnp.dot(y_ref[...], wo_ref[...], preferred_element_type=F32)
    hn = _rms(x1, g_ref[...]).astype(BF16)
    o_ref[...] = x1
    dff = w1_ref.shape[1]
    for c in range(dff // FFN_COL):
        cs = slice(c * FFN_COL, (c + 1) * FFN_COL)
        a = jnp.maximum(jnp.dot(hn, w1_ref[:, cs], preferred_element_type=F32), 0.0)
        o_ref[...] += jnp.dot((a * a).astype(BF16), w2_ref[cs, :], preferred_element_type=F32)
    if final:
        o_ref[...] = _rms(o_ref[...], fg_ref[...])


def _ffn_block(x2, y2, layer, p, final):
    n, dm = x2.shape
    tile = FFN_TILE
    stacked = [p[k] for k in ("w_out", "norm_ffn_g", "w_ff1", "w_ff2")]
    args = [x2, y2] + stacked + [p["final_norm_g"]]
    row = lambda w: pl.BlockSpec((tile, w), lambda i: (i, 0))
    in_specs = ([row(dm), row(y2.shape[1])] + [_layer_spec(a, layer) for a in stacked]
                + [_const_spec(p["final_norm_g"].shape)])
    return pl.pallas_call(
        functools.partial(_ffn_kernel, final=final),
        grid=(n // tile,),
        in_specs=in_specs,
        out_specs=row(dm),
        out_shape=jax.ShapeDtypeStruct((n, dm), F32),
        compiler_params=pltpu.CompilerParams(
            dimension_semantics=("arbitrary",), vmem_limit_bytes=VMEM_LIMIT),
        name="out_ffn",
    )(*args)


def kernel(x, norm_mix_g, w_in, s5_a_re, s5_a_im, s5_log_dt, s5_b_re, s5_b_im, s5_c_re, s5_c_im, s5_d, s5_w_glu, s5_b_glu, s5_out_g, ml_conv_w, ml_conv_b, ml_b_i, ml_b_f, ml_norm_g, w_out, norm_ffn_g, w_ff1, w_ff2, final_norm_g):
    bsz, seq, dm = x.shape
    depth = w_in.shape[0]
    ds = s5_w_glu.shape[2]
    dml = ml_norm_g.shape[1]
    nmain = ds + 4 * dml
    seg = MIX_TILE // SUBLANES
    bcat, ccat, ar, ai, asr, asi = jax.vmap(
        functools.partial(_s5_params, seg=seg))(s5_a_re, s5_a_im, s5_log_dt, s5_b_re, s5_b_im,
                                                s5_c_re, s5_c_im)
    row = lambda v: v.reshape(depth, 1, v.shape[-1])
    p = dict(
        norm_mix_g=row(norm_mix_g), w_in=w_in[:, :, :nmain].astype(BF16),
        wg=jnp.swapaxes(w_in[:, :, nmain:], 1, 2).astype(BF16),
        gb=jnp.concatenate([ml_b_i, ml_b_f], axis=1)[:, :, None],
        bcat=bcat, ar=ar, ai=ai, asr=asr, asi=asi, ccat=ccat, s5_d=row(s5_d.reshape(depth, ds)),
        w_glu=s5_w_glu.astype(BF16), b_glu=row(s5_b_glu), s5_out_g=row(s5_out_g),
        conv_w=ml_conv_w, conv_b=row(ml_conv_b), ml_norm_g=row(ml_norm_g),
        w_out=w_out.astype(BF16), norm_ffn_g=row(norm_ffn_g), w_ff1=w_ff1.astype(BF16),
        w_ff2=w_ff2.astype(BF16), final_norm_g=final_norm_g.reshape(1, dm))
    for l in range(depth):
        y = _mixer(x, l, p)
        x = _ffn_block(x.reshape(bsz * seq, dm), y.reshape(bsz * seq, y.shape[2]), l, p,
                       final=(l == depth - 1)).reshape(bsz, seq, dm)
    return x
```

```python
import functools
import math

import jax
import jax.numpy as jnp
from jax import lax
from jax.experimental import pallas as pl
from jax.experimental.pallas import tpu as pltpu

EPS = 1e-6
SSM_GROUP = 16
SSM_STATE = 64
ML_HEADS = 4
CONV_WIDTH = 4

SUBLANES = 8
LANES = 128
V7X_VMEM_BYTES = 64 * 1024 * 1024
VMEM_LIMIT = V7X_VMEM_BYTES - 8 * 1024 * 1024

MIX_TILE = 512
MIX_ROW_BLOCK = 128
ML_CHUNK = LANES
FFN_TILE = 1024
FFN_COL = 1024
S5_BLOCK_GROUPS = 8
S5_SCAN_PIECE = 16

F32 = jnp.float32
BF16 = jnp.bfloat16


def _rms(x, g):
    ms = jnp.mean(x * x, axis=-1, keepdims=True)
    return x * lax.rsqrt(ms + EPS) * g


def _sigmoid(x):
    return 1.0 / (1.0 + jnp.exp(-x))


def _log_sigmoid(x):
    return jnp.minimum(x, 0.0) - jnp.log(1.0 + jnp.exp(-jnp.abs(x)))


def _gelu_tanh(x):
    c = math.sqrt(2.0 / math.pi)
    return x * (0.5 * (1.0 + jnp.tanh(c * (x + 0.044715 * (x * x * x)))))


def _const_spec(shape):
    nd = len(shape)
    return pl.BlockSpec(shape, lambda *_: (0,) * nd, pipeline_mode=pl.Buffered(1))


def _run_interleaved(streams):
    totals = [sum(p[1] for p in s) for s in streams]
    pos = [0] * len(streams)
    spent = [0.0] * len(streams)
    done = set()
    while any(pos[i] < len(s) for i, s in enumerate(streams)):
        ready = [i for i, s in enumerate(streams)
                 if pos[i] < len(s) and all(r in done for r in s[pos[i]][2])]
        assert ready, "interleaving deadlock"
        i = min(ready, key=lambda k: spent[k] / totals[k])
        name, cost, _, fn = streams[i][pos[i]]
        fn()
        done.add(name)
        spent[i] += cost
        pos[i] += 1


def _s5_bu(j, rows, up_ref, bcat_ref, bu_ref):
    hw2 = bcat_ref.shape[2]
    gw = up_ref.shape[1] // bcat_ref.shape[0]
    bu_ref[j, rows, :] = jnp.dot(
        up_ref[rows, j * gw:(j + 1) * gw], bcat_ref[j], preferred_element_type=F32)


def _s5_scan_pieces(j, ar_ref, ai_ref, asr_ref, asi_ref, bu_ref, xs_ref, xb_ref, carry_ref, seg):
    nst = ar_ref.shape[1]
    hw = nst // (nst // (S5_BLOCK_GROUPS * SSM_STATE))
    cr = slice(0, hw)
    ci = slice(hw, 2 * hw)
    blk = slice(j * hw, (j + 1) * hw)
    pair = 2 * SUBLANES
    st = {}

    def coeffs():
        return (jnp.broadcast_to(ar_ref[:, blk], (SUBLANES, hw)),
                jnp.broadcast_to(ai_ref[:, blk], (SUBLANES, hw)))

    def pass1(s0):
        def fn():
            ar, ai = coeffs()
            sr, si = st.get("s", (jnp.zeros((SUBLANES, hw), F32),) * 2)
            for s in range(s0, s0 + S5_SCAN_PIECE):
                rows = slice(s * SUBLANES, (s + 1) * SUBLANES)
                sr, si = (ar * sr - ai * si + bu_ref[j, rows, cr],
                          ar * si + ai * sr + bu_ref[j, rows, ci])
                xs_ref[j, rows, cr] = sr
                xs_ref[j, rows, ci] = si
            st["s"] = (sr, si)
        return fn

    def carry():
        fr, fi = st["s"]
        asr = asr_ref[:, blk]
        asi = asi_ref[:, blk]
        rows_r = [carry_ref[0:1, blk]]
        rows_i = [carry_ref[1:2, blk]]
        for k in range(SUBLANES):
            pr, pi = rows_r[-1], rows_i[-1]
            rows_r.append(asr * pr - asi * pi + fr[k:k + 1])
            rows_i.append(asr * pi + asi * pr + fi[k:k + 1])
        carry_ref[0:1, blk] = rows_r[SUBLANES]
        carry_ref[1:2, blk] = rows_i[SUBLANES]
        st["w"] = (jnp.concatenate(rows_r[:SUBLANES], axis=0),
                   jnp.concatenate(rows_i[:SUBLANES], axis=0))

    def pass2(s0):
        def fn():
            ar, ai = coeffs()
            wr, wi = st["w"]
            for s in range(s0, s0 + S5_SCAN_PIECE, 2):
                r0 = slice(s * SUBLANES, (s + 1) * SUBLANES)
                r1 = slice((s + 1) * SUBLANES, (s + 2) * SUBLANES)
                w1r, w1i = ar * wr - ai * wi, ar * wi + ai * wr
                wr, wi = ar * w1r - ai * w1i, ar * w1i + ai * w1r
                both = slice(s * SUBLANES, (s + 2) * SUBLANES)
                xb_ref[j, both, cr] = jnp.concatenate(
                    [xs_ref[j, r0, cr] + w1r, xs_ref[j, r1, cr] + wr], axis=0).astype(BF16)
                xb_ref[j, both, ci] = jnp.concatenate(
                    [xs_ref[j, r0, ci] + w1i, xs_ref[j, r1, ci] + wi], axis=0).astype(BF16)
            st["w"] = (wr, wi)
        return fn

    pieces = []
    prev = []
    n = seg // S5_SCAN_PIECE
    for q in range(n):
        name = f"p1_{j}_{q}"
        pieces.append((name, 150, prev, pass1(q * S5_SCAN_PIECE)))
        prev = [name]
    pieces.append((f"carry{j}", 60, prev, carry))
    prev = [f"carry{j}"]
    for q in range(n):
        name = f"p2_{j}_{q}"
        pieces.append((name, 150, prev, pass2(q * S5_SCAN_PIECE)))
        prev = [name]
    return pieces, f"p1_{j}_{n - 1}", f"p2_{j}_{n - 1}"


def _s5_cproj(j, rows, xb_ref, ccat_ref, ys_ref):
    hw2 = ccat_ref.shape[1]
    gw = ccat_ref.shape[2]
    ys_ref[rows, j * gw:(j + 1) * gw] = jnp.dot(xb_ref[j, rows, :],
                                                ccat_ref[j], preferred_element_type=F32)


def _s5_back(rows, ys_ref, up_ref, d_ref, wglu_ref, bglu_ref, og_ref, zn_ref):
    y = ys_ref[rows, :] + d_ref[...] * up_ref[rows, :].astype(F32)
    z = _gelu_tanh(y)
    gate = _sigmoid(jnp.dot(z.astype(BF16), wglu_ref[...], preferred_element_type=F32)
                    + bglu_ref[...])
    zn_ref[rows, :] = _rms(z * gate, og_ref[...]).astype(BF16)


def _s5_params(a_re, a_im, log_dt, b_re, b_im, c_re, c_im, seg):
    dt = jnp.exp(log_dt)[:, None]
    mag = jnp.exp(a_re * dt)
    ab_re = mag * jnp.cos(a_im * dt)
    ab_im = mag * jnp.sin(a_im * dt)
    den = jnp.square(a_re) + jnp.square(a_im)
    zr = ab_re - 1.0
    s_re = (zr * a_re + ab_im * a_im) / den
    s_im = (ab_im * a_re - zr * a_im) / den
    bb_re = s_re[..., None] * b_re - s_im[..., None] * b_im
    bb_im = s_re[..., None] * b_im + s_im[..., None] * b_re
    mag_s = jnp.exp(a_re * dt * seg)
    as_re = mag_s * jnp.cos(a_im * dt * seg)
    as_im = mag_s * jnp.sin(a_im * dt * seg)

    g, n, p = bb_re.shape
    gb = S5_BLOCK_GROUPS
    nblk = g // gb
    eye = jnp.eye(gb, dtype=F32)

    def in_blocks(bb):
        t = bb.reshape(nblk, gb, n, p)
        m = jnp.einsum('jgnp,gh->jgphn', t, eye)
        return m.reshape(nblk, gb * p, gb * n)

    def out_blocks(cc):
        t = cc.reshape(nblk, gb, p, n)
        m = jnp.einsum('jgpn,gh->jgnhp', t, eye)
        return m.reshape(nblk, gb * n, gb * p)

    bcat = jnp.concatenate([in_blocks(bb_re), in_blocks(bb_im)], axis=2).astype(BF16)
    ccat = jnp.concatenate([out_blocks(c_re), -out_blocks(c_im)], axis=1).astype(BF16)
    flat = lambda v: v.reshape(1, g * n)
    return bcat, ccat, flat(ab_re), flat(ab_im), flat(as_re), flat(as_im)


def _ml_conv(c, part, nparts, ncol, cw_ref, cb_ref, xc_ref, q_ref, k_ref):
    tile, dml = q_ref.shape
    dh = dml // ML_HEADS
    tail = SUBLANES
    nrow = tile // nparts
    r0 = part * nrow
    rows = slice(r0, r0 + nrow)
    for sl in range(ncol // LANES):
        slab = c * (ncol // LANES) + sl
        cols = slice(slab * LANES, (slab + 1) * LANES)
        acc = cb_ref[:, cols]
        for j in range(CONV_WIDTH):
            off = r0 + tail - (CONV_WIDTH - 1) + j
            acc = acc + cw_ref[j:j + 1, cols] * xc_ref[slab, pl.ds(off, nrow, stride=1), :]
        if part == nparts - 1:
            xc_ref[slab, 0:tail, :] = xc_ref[slab, tile:tile + tail, :]
        qk = acc * _sigmoid(acc)
        if cols.stop <= dml:
            q_ref[rows, cols] = qk.astype(BF16)
        else:
            k_ref[rows, cols.start - dml:cols.stop - dml] = (
                qk * (1.0 / math.sqrt(dh))).astype(BF16)


def _ml_gate_rows(gt, a_ref):
    tile = gt.shape[1]
    chunk = ML_CHUNK
    lf = _log_sigmoid(gt)
    lane = lax.broadcasted_iota(jnp.int32, gt.shape, 1) % chunk
    neg_inf = jnp.float32(-jnp.inf)
    bc = lf
    shift = 1
    while shift < chunk:
        bc = bc + jnp.where(lane >= shift, pltpu.roll(bc, shift, axis=1), 0.0)
        shift *= 2
    a8 = gt - pltpu.roll(bc, ML_HEADS, axis=0)
    cm = a8
    shift = 1
    while shift < chunk:
        cm = jnp.maximum(cm, jnp.where(lane >= shift, pltpu.roll(cm, shift, axis=1), neg_inf))
        shift *= 2
    for c in range(tile // chunk):
        cs = slice(c * chunk, (c + 1) * chunk)
        a_ref[0, c] = a8[:, cs]
        a_ref[1, c] = cm[:, cs]
        a_ref[2, c] = bc[:, cs]


def _ml_chunk(c, q_ref, k_ref, v_ref, a_ref, h_ref, caug_ref, m_ref):
    tile, dml = q_ref.shape
    dh = dml // ML_HEADS
    chunk = ML_CHUNK
    neg_inf = jnp.float32(-jnp.inf)
    row_i = lax.broadcasted_iota(jnp.int32, (chunk, chunk), 0)
    col_i = lax.broadcasted_iota(jnp.int32, (chunk, chunk), 1)
    tril = col_i <= row_i
    ones_blk = jnp.ones((chunk, dh), BF16)

    def col_bcast(row):
        return jnp.broadcast_to(row, (chunk, chunk)).T

    rows = slice(c * chunk, (c + 1) * chunk)
    a_t, cm_t, bc_t = a_ref[0, c], a_ref[1, c], a_ref[2, c]
    for hd in range(ML_HEADS):
        sl = slice(hd * dh, (hd + 1) * dh)
        qh = q_ref[rows, sl]
        kh = k_ref[rows, sl]
        vh = v_ref[rows, sl]
        a_r = a_t[hd:hd + 1, :]
        cm_r = cm_t[hd:hd + 1, :]
        bc_r = bc_t[ML_HEADS + hd:ML_HEADS + hd + 1, :]
        m_prev = m_ref[hd:hd + 1, :]
        m_col = jnp.maximum(col_bcast(cm_r), m_prev)
        d_in = jnp.exp(jnp.where(tril, a_r - m_col, neg_inf))
        d_st = jnp.exp(m_prev - m_col)
        e = jnp.exp(-(col_bcast(bc_r) + m_col))
        s = lax.dot_general(qh, kh, (((1,), (1,)), ((), ())),
                            preferred_element_type=F32) * d_in
        lhs = jnp.concatenate([s, qh.astype(F32) * d_st], axis=1).astype(BF16)
        vaug = jnp.concatenate([vh, ones_blk], axis=1)
        caug = caug_ref[hd]
        rhs = jnp.concatenate([vaug, caug.astype(BF16)], axis=0)
        na = jnp.dot(lhs, rhs, preferred_element_type=F32)
        h_ref[rows, sl] = na[:, :dh] / jnp.maximum(jnp.abs(na[:, dh:]), e)
        wk_r = d_in[chunk - 1:chunk, :]
        decay = d_st[chunk - 1:chunk, :]
        kwt = (kh.T.astype(F32) * wk_r).astype(BF16)
        upd = jnp.dot(kwt, vaug, preferred_element_type=F32)
        caug_ref[hd] = jnp.concatenate([decay, decay], axis=1) * caug + upd
        m_ref[hd:hd + 1, :] = (bc_r[:, chunk - 1:chunk]
                               + jnp.maximum(cm_r[:, chunk - 1:chunk], m_prev))


def _ml_back(hd, o_ref, h_ref, ng_ref):
    dh = o_ref.shape[1] // ML_HEADS
    sl = slice(hd * dh, (hd + 1) * dh)
    hv = _sigmoid(o_ref[:, sl]) * h_ref[:, sl]
    return _rms(hv, ng_ref[:, sl]).astype(BF16)


def _mix_kernel(x_ref, g_ref, win_ref, wg_ref, gb_ref,
                p_ref, pt_ref, bcat_ref, ar_ref, ai_ref, asr_ref, asi_ref, ccat_ref, d_ref,
                wglu_ref, bglu_ref, og_ref, cw_ref, cb_ref, ng_ref,
                y_ref,
                hb_ref, ub_ref, up2_ref, bu_ref, xs_ref, xb_ref, ys_ref, zn_ref, carry_ref,
                xc_ref, q2_ref, k2_ref, v2_ref, o2_ref, h_ref, a2_ref, caug_ref, m_ref,
                *, seg, tiles_per_seq, front):
    tile = hb_ref.shape[0]
    ds = up2_ref.shape[2]
    dml = q2_ref.shape[2]
    nblk = bcat_ref.shape[0]
    ncol = 2 * dml // nblk
    tail = SUBLANES
    step = pl.program_id(0)
    back = 1 - front

    @pl.when(step == 0)
    def _():
        up2_ref[...] = jnp.zeros_like(up2_ref)
        bu_ref[...] = jnp.zeros_like(bu_ref)
        q2_ref[...] = jnp.zeros_like(q2_ref)
        k2_ref[...] = jnp.zeros_like(k2_ref)
        v2_ref[...] = jnp.zeros_like(v2_ref)
        o2_ref[...] = jnp.zeros_like(o2_ref)
        a2_ref[...] = jnp.zeros_like(a2_ref)
        carry_ref[...] = jnp.zeros_like(carry_ref)
        caug_ref[...] = jnp.zeros_like(caug_ref)
        m_ref[...] = jnp.zeros_like(m_ref)

    @pl.when(step % tiles_per_seq == 0)
    def _():
        xc_ref[:, 0:tail, :] = jnp.zeros((xc_ref.shape[0], tail, LANES), F32)

    @pl.when(step % tiles_per_seq == 1 % tiles_per_seq)
    def _():
        carry_ref[...] = jnp.zeros_like(carry_ref)
        caug_ref[...] = jnp.zeros_like(caug_ref)
        m_ref[...] = jnp.zeros_like(m_ref)

    up_f, q_f, k_f, v_f, o_f, a_f = (r.at[front] for r in
                                     (up2_ref, q2_ref, k2_ref, v2_ref, o2_ref, a2_ref))
    up_b, q_b, k_b, v_b, o_b, a_b = (r.at[back] for r in
                                     (up2_ref, q2_ref, k2_ref, v2_ref, o2_ref, a2_ref))

    rb = MIX_ROW_BLOCK
    nrb = tile // rb
    rblocks = [slice(r * rb, (r + 1) * rb) for r in range(nrb)]

    def proj(rows, lo, hi):
        return jnp.dot(hb_ref[rows, :], win_ref[:, lo:hi], preferred_element_type=F32)

    def f_rms(rows):
        def fn():
            hb_ref[rows, :] = _rms(x_ref[0, rows, :], g_ref[...]).astype(BF16)
        return fn

    def f_u(rows):
        def fn():
            ub_ref[rows, :] = proj(rows, 0, ds).astype(BF16)
        return fn

    def f_perm(rows):
        def fn():
            up_f[rows, :] = jnp.dot(p_ref[rows, :], ub_ref[...],
                                    preferred_element_type=F32).astype(BF16)
        return fn

    def f_gates():
        gt = lax.dot_general(wg_ref[...], hb_ref[...], (((1,), (1,)), ((), ())),
                             preferred_element_type=F32) + gb_ref[...]
        _ml_gate_rows(gt, a_f)

    def f_qk(c, r):
        def fn():
            v = proj(rblocks[r], ds + c * ncol, ds + (c + 1) * ncol)
            for sl in range(ncol // LANES):
                xc_ref[c * (ncol // LANES) + sl, tail + r * rb:tail + (r + 1) * rb, :] = (
                    v[:, sl * LANES:(sl + 1) * LANES])
        return fn

    def f_v(rows):
        def fn():
            v_f[rows, :] = proj(rows, ds + 2 * dml, ds + 3 * dml).astype(BF16)
        return fn

    def f_o(rows):
        def fn():
            o_f[rows, :] = proj(rows, ds + 3 * dml, ds + 4 * dml)
        return fn

    scans = [_s5_scan_pieces(j, ar_ref, ai_ref, asr_ref, asi_ref, bu_ref, xs_ref, xb_ref,
                             carry_ref, seg) for j in range(nblk)]

    def b_unperm(rows):
        def fn():
            y_ref[0, rows, :ds] = jnp.dot(pt_ref[rows, :], zn_ref[...],
                                          preferred_element_type=F32).astype(BF16)
        return fn

    def b_ml(hd):
        def fn():
            dh = dml // ML_HEADS
            y_ref[0, :, ds + hd * dh:ds + (hd + 1) * dh] = _ml_back(hd, o_b, h_ref, ng_ref)
        return fn

    nchunk = tile // ML_CHUNK
    all_rms = [f"rms{r}" for r in range(nrb)]
    mxu_stream = [(f"u{r}", 128, [f"rms{r}"], f_u(rblocks[r])) for r in range(nrb)]
    mxu_stream += [(f"perm{r}", 64, [f"u{q}" for q in range(nrb)], f_perm(rblocks[r]))
                   for r in range(nrb)]
    mxu_stream += [(f"qk0_{r}", 128, [f"rms{r}"], f_qk(0, r)) for r in range(nrb)]
    for c in range(nblk):
        if c + 1 < nblk:
            mxu_stream += [(f"qk{c + 1}_{r}", 128, [f"rms{r}"], f_qk(c + 1, r))
                           for r in range(nrb)]
        mxu_stream += [(f"bu{c}_{r}", 150, [f"perm{r}", scans[c][1]], functools.partial(
            _s5_bu, c, rblocks[r], up_f, bcat_ref, bu_ref)) for r in range(nrb)]
        if c == 0:
            mxu_stream += [(f"v{r}", 128, [f"rms{r}"], f_v(rblocks[r])) for r in range(nrb)]
        if c == 1:
            mxu_stream += [(f"o{r}", 256, [f"rms{r}"], f_o(rblocks[r])) for r in range(nrb)]

    vpu_stream = [(f"rms{r}", 150, [], f_rms(rblocks[r])) for r in range(nrb)]
    vpu_stream.append(("gates", 300, all_rms, f_gates))
    for j in range(nblk):
        vpu_stream.extend(scans[j][0])
        for part in range(nrb):
            vpu_stream.append((f"conv{j}_{part}", 275, [f"qk{j}_{part}"], functools.partial(
                _ml_conv, j, part, nrb, ncol, cw_ref, cb_ref, xc_ref, q_f, k_f)))
    _run_interleaved([mxu_stream, vpu_stream])

    chunk_stream = [(f"chunk{c}", 1500, [], functools.partial(
        _ml_chunk, c, q_b, k_b, v_b, a_b, h_ref, caug_ref, m_ref)) for c in range(nchunk)]
    chunk_stream += [(f"mlback{hd}", 300, [], b_ml(hd)) for hd in range(ML_HEADS)]
    out_stream = [(f"cproj{c}_{r}", 130, [], functools.partial(
        _s5_cproj, c, rblocks[r], xb_ref, ccat_ref, ys_ref))
        for c in range(nblk) for r in range(nrb)]
    out_stream += [(f"s5back{r}", 350, [], functools.partial(
        _s5_back, rblocks[r], ys_ref, up_b, d_ref, wglu_ref, bglu_ref, og_ref, zn_ref))
        for r in range(nrb)]
    out_stream += [(f"unperm{r}", 64, [], b_unperm(rblocks[r])) for r in range(nrb)]
    _run_interleaved([chunk_stream, out_stream])


def _mix_entry(*refs, **static):
    step = pl.program_id(0)
    for parity in range(2):
        pl.when(step % 2 == parity)(functools.partial(_mix_kernel, *refs, front=parity, **static))


def _layer_spec(arr, layer, block=None):
    block = tuple(arr.shape[1:]) if block is None else block
    nd = len(block)
    return pl.BlockSpec((None,) + block, lambda *_: (layer,) + (0,) * nd,
                        pipeline_mode=pl.Buffered(1))


def _mixer(x, layer, p):
    bsz, seq, dm = x.shape
    tile = MIX_TILE
    seg = tile // SUBLANES
    tiles_per_seq = seq // tile
    ntiles = bsz * tiles_per_seq
    ds = p["w_glu"].shape[2]
    dml = p["ml_norm_g"].shape[2]
    dh = dml // ML_HEADS
    nmain = ds + 4 * dml
    nst = p["ar"].shape[2]
    nblk = p["bcat"].shape[1]
    r = jnp.arange(tile)
    src = (r % SUBLANES) * seg + r // SUBLANES
    perm = (jnp.arange(tile)[None, :] == src[:, None]).astype(BF16)
    stacked = [p[k] for k in ("norm_mix_g", "w_in", "wg", "gb")]
    consts = [perm, perm.T]
    stacked2 = [p[k] for k in ("bcat", "ar", "ai", "asr", "asi", "ccat", "s5_d", "w_glu", "b_glu",
                               "s5_out_g", "conv_w", "conv_b", "ml_norm_g")]
    args = [x.reshape(ntiles, tile, dm)] + stacked + consts + stacked2
    in_specs = [pl.BlockSpec((1, tile, dm), lambda s: (jnp.minimum(s, ntiles - 1), 0, 0))]
    in_specs += [_layer_spec(a, layer, (dm, nmain) if a is p["w_in"] else None) for a in stacked]
    in_specs += [_const_spec(a.shape) for a in consts]
    in_specs += [_layer_spec(a, layer) for a in stacked2]
    y = pl.pallas_call(
        functools.partial(_mix_entry, seg=seg, tiles_per_seq=tiles_per_seq),
        grid=(ntiles + 1,),
        in_specs=in_specs,
        out_specs=pl.BlockSpec((1, tile, ds + dml), lambda s: (jnp.maximum(s - 1, 0), 0, 0)),
        out_shape=jax.ShapeDtypeStruct((ntiles, tile, ds + dml), BF16),
        scratch_shapes=[
            pltpu.VMEM((tile, dm), BF16),
            pltpu.VMEM((tile, ds), BF16),
            pltpu.VMEM((2, tile, ds), BF16),
            pltpu.VMEM((nblk, tile, 2 * nst // nblk), F32),
            pltpu.VMEM((nblk, tile, 2 * nst // nblk), F32),
            pltpu.VMEM((nblk, tile, 2 * nst // nblk), BF16),
            pltpu.VMEM((tile, ds), F32),
            pltpu.VMEM((tile, ds), BF16),
            pltpu.VMEM((2, nst), F32),
            pltpu.VMEM((2 * dml // LANES, tile + SUBLANES, LANES), F32),
            pltpu.VMEM((2, tile, dml), BF16),
            pltpu.VMEM((2, tile, dml), BF16),
            pltpu.VMEM((2, tile, dml), BF16),
            pltpu.VMEM((2, tile, dml), F32),
            pltpu.VMEM((tile, dml), F32),
            pltpu.VMEM((2, 3, tile // ML_CHUNK, 2 * ML_HEADS, ML_CHUNK), F32),
            pltpu.VMEM((ML_HEADS, dh, 2 * dh), F32),
            pltpu.VMEM((2 * ML_HEADS, LANES), F32),
        ],
        compiler_params=pltpu.CompilerParams(
            dimension_semantics=("arbitrary",), vmem_limit_bytes=VMEM_LIMIT),
        name="mixer",
    )(*args)
    return y.reshape(bsz, seq, ds + dml)


def _ffn_kernel(x_ref, y_ref, wo_ref, g_ref, w1_ref, w2_ref, fg_ref, o_ref, *, final):
    x1 = x_ref[...] + jnp.dot(y_ref[...], wo_ref[...], preferred_element_type=F32)
    hn = _rms(x1, g_ref[...]).astype(BF16)
    o_ref[...] = x1
    dff = w1_ref.shape[1]
    for c in range(dff // FFN_COL):
        cs = slice(c * FFN_COL, (c + 1) * FFN_COL)
        a = jnp.maximum(jnp.dot(hn, w1_ref[:, cs], preferred_element_type=F32), 0.0)
        o_ref[...] += jnp.dot((a * a).astype(BF16), w2_ref[cs, :], preferred_element_type=F32)
    if final:
        o_ref[...] = _rms(o_ref[...], fg_ref[...])


def _ffn_block(x2, y2, layer, p, final):
    n, dm = x2.shape
    tile = FFN_TILE
    stacked = [p[k] for k in ("w_out", "norm_ffn_g", "w_ff1", "w_ff2")]
    args = [x2, y2] + stacked + [p["final_norm_g"]]
    row = lambda w: pl.BlockSpec((tile, w), lambda i: (i, 0))
    in_specs = ([row(dm), row(y2.shape[1])] + [_layer_spec(a, layer) for a in stacked]
                + [_const_spec(p["final_norm_g"].shape)])
    return pl.pallas_call(
        functools.partial(_ffn_kernel, final=final),
        grid=(n // tile,),
        in_specs=in_specs,
        out_specs=row(dm),
        out_shape=jax.ShapeDtypeStruct((n, dm), F32),
        compiler_params=pltpu.CompilerParams(
            dimension_semantics=("arbitrary",), vmem_limit_bytes=VMEM_LIMIT),
        name="out_ffn",
    )(*args)


def kernel(x, norm_mix_g, w_in, s5_a_re, s5_a_im, s5_log_dt, s5_b_re, s5_b_im, s5_c_re, s5_c_im, s5_d, s5_w_glu, s5_b_glu, s5_out_g, ml_conv_w, ml_conv_b, ml_b_i, ml_b_f, ml_norm_g, w_out, norm_ffn_g, w_ff1, w_ff2, final_norm_g):
    bsz, seq, dm = x.shape
    depth = w_in.shape[0]
    ds = s5_w_glu.shape[2]
    dml = ml_norm_g.shape[1]
    nmain = ds + 4 * dml
    seg = MIX_TILE // SUBLANES
    bcat, ccat, ar, ai, asr, asi = jax.vmap(
        functools.partial(_s5_params, seg=seg))(s5_a_re, s5_a_im, s5_log_dt, s5_b_re, s5_b_im,
                                                s5_c_re, s5_c_im)
    row = lambda v: v.reshape(depth, 1, v.shape[-1])
    p = dict(
        norm_mix_g=row(norm_mix_g), w_in=w_in[:, :, :nmain].astype(BF16),
        wg=jnp.swapaxes(w_in[:, :, nmain:], 1, 2).astype(BF16),
        gb=jnp.concatenate([ml_b_i, ml_b_f], axis=1)[:, :, None],
        bcat=bcat, ar=ar, ai=ai, asr=asr, asi=asi, ccat=ccat, s5_d=row(s5_d.reshape(depth, ds)),
        w_glu=s5_w_glu.astype(BF16), b_glu=row(s5_b_glu), s5_out_g=row(s5_out_g),
        conv_w=ml_conv_w, conv_b=row(ml_conv_b), ml_norm_g=row(ml_norm_g),
        w_out=w_out.astype(BF16), norm_ffn_g=row(norm_ffn_g), w_ff1=w_ff1.astype(BF16),
        w_ff2=w_ff2.astype(BF16), final_norm_g=final_norm_g.reshape(1, dm))
    for l in range(depth):
        y = _mixer(x, l, p)
        x = _ffn_block(x.reshape(bsz * seq, dm), y.reshape(bsz * seq, y.shape[2]), l, p,
                       final=(l == depth - 1)).reshape(bsz, seq, dm)
    return x
```

```python
import functools
import math

import jax
import jax.numpy as jnp
from jax import lax
from jax.experimental import pallas as pl
from jax.experimental.pallas import tpu as pltpu

EPS = 1e-6
SSM_GROUP = 16
SSM_STATE = 64
ML_HEADS = 4
CONV_WIDTH = 4

SUBLANES = 8
LANES = 128
V7X_VMEM_BYTES = 64 * 1024 * 1024
VMEM_LIMIT = V7X_VMEM_BYTES - 8 * 1024 * 1024

MIX_TILE = 512
MIX_ROW_BLOCK = 128
ML_CHUNK = LANES
FFN_TILE = 512
FFN_COL = 2048
S5_BLOCK_GROUPS = 8
S5_SCAN_PIECE = 16

F32 = jnp.float32
BF16 = jnp.bfloat16


def _rms(x, g):
    ms = jnp.mean(x * x, axis=-1, keepdims=True)
    return x * lax.rsqrt(ms + EPS) * g


def _sigmoid(x):
    return 1.0 / (1.0 + jnp.exp(-x))


def _log_sigmoid(x):
    return jnp.minimum(x, 0.0) - jnp.log(1.0 + jnp.exp(-jnp.abs(x)))


def _gelu_tanh(x):
    c = math.sqrt(2.0 / math.pi)
    return x * (0.5 * (1.0 + jnp.tanh(c * (x + 0.044715 * (x * x * x)))))


def _const_spec(shape):
    nd = len(shape)
    return pl.BlockSpec(shape, lambda *_: (0,) * nd, pipeline_mode=pl.Buffered(1))


def _run_interleaved(streams):
    totals = [sum(p[1] for p in s) for s in streams]
    pos = [0] * len(streams)
    spent = [0.0] * len(streams)
    done = set()
    while any(pos[i] < len(s) for i, s in enumerate(streams)):
        ready = [i for i, s in enumerate(streams)
                 if pos[i] < len(s) and all(r in done for r in s[pos[i]][2])]
        assert ready, "interleaving deadlock"
        i = min(ready, key=lambda k: spent[k] / totals[k])
        name, cost, _, fn = streams[i][pos[i]]
        fn()
        done.add(name)
        spent[i] += cost
        pos[i] += 1


def _s5_bu(j, rows, up_ref, bcat_ref, bu_ref):
    hw2 = bcat_ref.shape[2]
    gw = up_ref.shape[1] // bcat_ref.shape[0]
    bu_ref[j, rows, :] = jnp.dot(
        up_ref[rows, j * gw:(j + 1) * gw], bcat_ref[j], preferred_element_type=F32)


def _s5_scan_pieces(j, ar_ref, ai_ref, asr_ref, asi_ref, bu_ref, xs_ref, xb_ref, carry_ref, seg):
    nst = ar_ref.shape[1]
    hw = nst // (nst // (S5_BLOCK_GROUPS * SSM_STATE))
    cr = slice(0, hw)
    ci = slice(hw, 2 * hw)
    blk = slice(j * hw, (j + 1) * hw)
    pair = 2 * SUBLANES
    st = {}

    def coeffs():
        return (jnp.broadcast_to(ar_ref[:, blk], (SUBLANES, hw)),
                jnp.broadcast_to(ai_ref[:, blk], (SUBLANES, hw)))

    def pass1(s0):
        def fn():
            ar, ai = coeffs()
            sr, si = st.get("s", (jnp.zeros((SUBLANES, hw), F32),) * 2)
            for s in range(s0, s0 + S5_SCAN_PIECE):
                rows = slice(s * SUBLANES, (s + 1) * SUBLANES)
                sr, si = (ar * sr - ai * si + bu_ref[j, rows, cr],
                          ar * si + ai * sr + bu_ref[j, rows, ci])
                xs_ref[j, rows, cr] = sr
                xs_ref[j, rows, ci] = si
            st["s"] = (sr, si)
        return fn

    def carry():
        fr, fi = st["s"]
        asr = asr_ref[:, blk]
        asi = asi_ref[:, blk]
        rows_r = [carry_ref[0:1, blk]]
        rows_i = [carry_ref[1:2, blk]]
        for k in range(SUBLANES):
            pr, pi = rows_r[-1], rows_i[-1]
            rows_r.append(asr * pr - asi * pi + fr[k:k + 1])
            rows_i.append(asr * pi + asi * pr + fi[k:k + 1])
        carry_ref[0:1, blk] = rows_r[SUBLANES]
        carry_ref[1:2, blk] = rows_i[SUBLANES]
        st["w"] = (jnp.concatenate(rows_r[:SUBLANES], axis=0),
                   jnp.concatenate(rows_i[:SUBLANES], axis=0))

    def pass2(s0):
        def fn():
            ar, ai = coeffs()
            wr, wi = st["w"]
            for s in range(s0, s0 + S5_SCAN_PIECE, 2):
                r0 = slice(s * SUBLANES, (s + 1) * SUBLANES)
                r1 = slice((s + 1) * SUBLANES, (s + 2) * SUBLANES)
                w1r, w1i = ar * wr - ai * wi, ar * wi + ai * wr
                wr, wi = ar * w1r - ai * w1i, ar * w1i + ai * w1r
                both = slice(s * SUBLANES, (s + 2) * SUBLANES)
                xb_ref[j, both, cr] = jnp.concatenate(
                    [xs_ref[j, r0, cr] + w1r, xs_ref[j, r1, cr] + wr], axis=0).astype(BF16)
                xb_ref[j, both, ci] = jnp.concatenate(
                    [xs_ref[j, r0, ci] + w1i, xs_ref[j, r1, ci] + wi], axis=0).astype(BF16)
            st["w"] = (wr, wi)
        return fn

    pieces = []
    prev = []
    n = seg // S5_SCAN_PIECE
    for q in range(n):
        name = f"p1_{j}_{q}"
        pieces.append((name, 150, prev, pass1(q * S5_SCAN_PIECE)))
        prev = [name]
    pieces.append((f"carry{j}", 60, prev, carry))
    prev = [f"carry{j}"]
    for q in range(n):
        name = f"p2_{j}_{q}"
        pieces.append((name, 150, prev, pass2(q * S5_SCAN_PIECE)))
        prev = [name]
    return pieces, f"p1_{j}_{n - 1}", f"p2_{j}_{n - 1}"


def _s5_cproj(j, rows, xb_ref, ccat_ref, ys_ref):
    hw2 = ccat_ref.shape[1]
    gw = ccat_ref.shape[2]
    ys_ref[rows, j * gw:(j + 1) * gw] = jnp.dot(xb_ref[j, rows, :],
                                                ccat_ref[j], preferred_element_type=F32)


def _s5_back(rows, ys_ref, up_ref, d_ref, wglu_ref, bglu_ref, og_ref, zn_ref):
    y = ys_ref[rows, :] + d_ref[...] * up_ref[rows, :].astype(F32)
    z = _gelu_tanh(y)
    gate = _sigmoid(jnp.dot(z.astype(BF16), wglu_ref[...], preferred_element_type=F32)
                    + bglu_ref[...])
    zn_ref[rows, :] = _rms(z * gate, og_ref[...]).astype(BF16)


def _s5_params(a_re, a_im, log_dt, b_re, b_im, c_re, c_im, seg):
    dt = jnp.exp(log_dt)[:, None]
    mag = jnp.exp(a_re * dt)
    ab_re = mag * jnp.cos(a_im * dt)
    ab_im = mag * jnp.sin(a_im * dt)
    den = jnp.square(a_re) + jnp.square(a_im)
    zr = ab_re - 1.0
    s_re = (zr * a_re + ab_im * a_im) / den
    s_im = (ab_im * a_re - zr * a_im) / den
    bb_re = s_re[..., None] * b_re - s_im[..., None] * b_im
    bb_im = s_re[..., None] * b_im + s_im[..., None] * b_re
    mag_s = jnp.exp(a_re * dt * seg)
    as_re = mag_s * jnp.cos(a_im * dt * seg)
    as_im = mag_s * jnp.sin(a_im * dt * seg)

    g, n, p = bb_re.shape
    gb = S5_BLOCK_GROUPS
    nblk = g // gb
    eye = jnp.eye(gb, dtype=F32)

    def in_blocks(bb):
        t = bb.reshape(nblk, gb, n, p)
        m = jnp.einsum('jgnp,gh->jgphn', t, eye)
        return m.reshape(nblk, gb * p, gb * n)

    def out_blocks(cc):
        t = cc.reshape(nblk, gb, p, n)
        m = jnp.einsum('jgpn,gh->jgnhp', t, eye)
        return m.reshape(nblk, gb * n, gb * p)

    bcat = jnp.concatenate([in_blocks(bb_re), in_blocks(bb_im)], axis=2).astype(BF16)
    ccat = jnp.concatenate([out_blocks(c_re), -out_blocks(c_im)], axis=1).astype(BF16)
    flat = lambda v: v.reshape(1, g * n)
    return bcat, ccat, flat(ab_re), flat(ab_im), flat(as_re), flat(as_im)


def _ml_conv(c, part, nparts, ncol, cw_ref, cb_ref, xc_ref, q_ref, k_ref):
    tile, dml = q_ref.shape
    dh = dml // ML_HEADS
    tail = SUBLANES
    nrow = tile // nparts
    r0 = part * nrow
    rows = slice(r0, r0 + nrow)
    for sl in range(ncol // LANES):
        slab = c * (ncol // LANES) + sl
        cols = slice(slab * LANES, (slab + 1) * LANES)
        acc = cb_ref[:, cols]
        for j in range(CONV_WIDTH):
            off = r0 + tail - (CONV_WIDTH - 1) + j
            acc = acc + cw_ref[j:j + 1, cols] * xc_ref[slab, pl.ds(off, nrow, stride=1), :]
        if part == nparts - 1:
            xc_ref[slab, 0:tail, :] = xc_ref[slab, tile:tile + tail, :]
        qk = acc * _sigmoid(acc)
        if cols.stop <= dml:
            q_ref[rows, cols] = qk.astype(BF16)
        else:
            k_ref[rows, cols.start - dml:cols.stop - dml] = (
                qk * (1.0 / math.sqrt(dh))).astype(BF16)


def _ml_gate_rows(gt, a_ref):
    tile = gt.shape[1]
    chunk = ML_CHUNK
    lf = _log_sigmoid(gt)
    lane = lax.broadcasted_iota(jnp.int32, gt.shape, 1) % chunk
    neg_inf = jnp.float32(-jnp.inf)
    bc = lf
    shift = 1
    while shift < chunk:
        bc = bc + jnp.where(lane >= shift, pltpu.roll(bc, shift, axis=1), 0.0)
        shift *= 2
    a8 = gt - pltpu.roll(bc, ML_HEADS, axis=0)
    cm = a8
    shift = 1
    while shift < chunk:
        cm = jnp.maximum(cm, jnp.where(lane >= shift, pltpu.roll(cm, shift, axis=1), neg_inf))
        shift *= 2
    for c in range(tile // chunk):
        cs = slice(c * chunk, (c + 1) * chunk)
        a_ref[0, c] = a8[:, cs]
        a_ref[1, c] = cm[:, cs]
        a_ref[2, c] = bc[:, cs]


def _ml_chunk(c, q_ref, k_ref, v_ref, a_ref, h_ref, caug_ref, m_ref):
    tile, dml = q_ref.shape
    dh = dml // ML_HEADS
    chunk = ML_CHUNK
    neg_inf = jnp.float32(-jnp.inf)
    row_i = lax.broadcasted_iota(jnp.int32, (chunk, chunk), 0)
    col_i = lax.broadcasted_iota(jnp.int32, (chunk, chunk), 1)
    tril = col_i <= row_i
    ones_blk = jnp.ones((chunk, dh), BF16)

    def col_bcast(row):
        return jnp.broadcast_to(row, (chunk, chunk)).T

    rows = slice(c * chunk, (c + 1) * chunk)
    a_t, cm_t, bc_t = a_ref[0, c], a_ref[1, c], a_ref[2, c]
    for hd in range(ML_HEADS):
        sl = slice(hd * dh, (hd + 1) * dh)
        qh = q_ref[rows, sl]
        kh = k_ref[rows, sl]
        vh = v_ref[rows, sl]
        a_r = a_t[hd:hd + 1, :]
        cm_r = cm_t[hd:hd + 1, :]
        bc_r = bc_t[ML_HEADS + hd:ML_HEADS + hd + 1, :]
        m_prev = m_ref[hd:hd + 1, :]
        m_col = jnp.maximum(col_bcast(cm_r), m_prev)
        d_in = jnp.exp(jnp.where(tril, a_r - m_col, neg_inf))
        d_st = jnp.exp(m_prev - m_col)
        e = jnp.exp(-(col_bcast(bc_r) + m_col))
        s = lax.dot_general(qh, kh, (((1,), (1,)), ((), ())),
                            preferred_element_type=F32) * d_in
        lhs = jnp.concatenate([s, qh.astype(F32) * d_st], axis=1).astype(BF16)
        vaug = jnp.concatenate([vh, ones_blk], axis=1)
        caug = caug_ref[hd]
        rhs = jnp.concatenate([vaug, caug.astype(BF16)], axis=0)
        na = jnp.dot(lhs, rhs, preferred_element_type=F32)
        h_ref[rows, sl] = na[:, :dh] / jnp.maximum(jnp.abs(na[:, dh:]), e)
        wk_r = d_in[chunk - 1:chunk, :]
        decay = d_st[chunk - 1:chunk, :]
        kwt = (kh.T.astype(F32) * wk_r).astype(BF16)
        upd = jnp.dot(kwt, vaug, preferred_element_type=F32)
        caug_ref[hd] = jnp.concatenate([decay, decay], axis=1) * caug + upd
        m_ref[hd:hd + 1, :] = (bc_r[:, chunk - 1:chunk]
                               + jnp.maximum(cm_r[:, chunk - 1:chunk], m_prev))


def _ml_back(hd, o_ref, h_ref, ng_ref):
    dh = o_ref.shape[1] // ML_HEADS
    sl = slice(hd * dh, (hd + 1) * dh)
    hv = _sigmoid(o_ref[:, sl]) * h_ref[:, sl]
    return _rms(hv, ng_ref[:, sl]).astype(BF16)


def _mix_kernel(x_ref, g_ref, win_ref, wg_ref, gb_ref,
                p_ref, pt_ref, bcat_ref, ar_ref, ai_ref, asr_ref, asi_ref, ccat_ref, d_ref,
                wglu_ref, bglu_ref, og_ref, cw_ref, cb_ref, ng_ref,
                y_ref,
                hb_ref, ub_ref, up2_ref, bu_ref, xs_ref, xb_ref, ys_ref, zn_ref, carry_ref,
                xc_ref, q2_ref, k2_ref, v2_ref, o2_ref, h_ref, a2_ref, caug_ref, m_ref,
                *, seg, tiles_per_seq, front):
    tile = hb_ref.shape[0]
    ds = up2_ref.shape[2]
    dml = q2_ref.shape[2]
    nblk = bcat_ref.shape[0]
    ncol = 2 * dml // nblk
    tail = SUBLANES
    step = pl.program_id(0)
    back = 1 - front

    @pl.when(step == 0)
    def _():
        up2_ref[...] = jnp.zeros_like(up2_ref)
        bu_ref[...] = jnp.zeros_like(bu_ref)
        q2_ref[...] = jnp.zeros_like(q2_ref)
        k2_ref[...] = jnp.zeros_like(k2_ref)
        v2_ref[...] = jnp.zeros_like(v2_ref)
        o2_ref[...] = jnp.zeros_like(o2_ref)
        a2_ref[...] = jnp.zeros_like(a2_ref)
        carry_ref[...] = jnp.zeros_like(carry_ref)
        caug_ref[...] = jnp.zeros_like(caug_ref)
        m_ref[...] = jnp.zeros_like(m_ref)

    @pl.when(step % tiles_per_seq == 0)
    def _():
        xc_ref[:, 0:tail, :] = jnp.zeros((xc_ref.shape[0], tail, LANES), F32)

    @pl.when(step % tiles_per_seq == 1 % tiles_per_seq)
    def _():
        carry_ref[...] = jnp.zeros_like(carry_ref)
        caug_ref[...] = jnp.zeros_like(caug_ref)
        m_ref[...] = jnp.zeros_like(m_ref)

    up_f, q_f, k_f, v_f, o_f, a_f = (r.at[front] for r in
                                     (up2_ref, q2_ref, k2_ref, v2_ref, o2_ref, a2_ref))
    up_b, q_b, k_b, v_b, o_b, a_b = (r.at[back] for r in
                                     (up2_ref, q2_ref, k2_ref, v2_ref, o2_ref, a2_ref))

    rb = MIX_ROW_BLOCK
    nrb = tile // rb
    rblocks = [slice(r * rb, (r + 1) * rb) for r in range(nrb)]

    def proj(rows, lo, hi):
        return jnp.dot(hb_ref[rows, :], win_ref[:, lo:hi], preferred_element_type=F32)

    def f_rms(rows):
        def fn():
            hb_ref[rows, :] = _rms(x_ref[0, rows, :], g_ref[...]).astype(BF16)
        return fn

    def f_u(rows):
        def fn():
            ub_ref[rows, :] = proj(rows, 0, ds).astype(BF16)
        return fn

    def f_perm(rows):
        def fn():
            up_f[rows, :] = jnp.dot(p_ref[rows, :], ub_ref[...],
                                    preferred_element_type=F32).astype(BF16)
        return fn

    def f_gates():
        gt = lax.dot_general(wg_ref[...], hb_ref[...], (((1,), (1,)), ((), ())),
                             preferred_element_type=F32) + gb_ref[...]
        _ml_gate_rows(gt, a_f)

    def f_qk(c, r):
        def fn():
            v = proj(rblocks[r], ds + c * ncol, ds + (c + 1) * ncol)
            for sl in range(ncol // LANES):
                xc_ref[c * (ncol // LANES) + sl, tail + r * rb:tail + (r + 1) * rb, :] = (
                    v[:, sl * LANES:(sl + 1) * LANES])
        return fn

    def f_v(rows):
        def fn():
            v_f[rows, :] = proj(rows, ds + 2 * dml, ds + 3 * dml).astype(BF16)
        return fn

    def f_o(rows):
        def fn():
            o_f[rows, :] = proj(rows, ds + 3 * dml, ds + 4 * dml)
        return fn

    scans = [_s5_scan_pieces(j, ar_ref, ai_ref, asr_ref, asi_ref, bu_ref, xs_ref, xb_ref,
                             carry_ref, seg) for j in range(nblk)]

    def b_unperm(rows):
        def fn():
            y_ref[0, rows, :ds] = jnp.dot(pt_ref[rows, :], zn_ref[...],
                                          preferred_element_type=F32).astype(BF16)
        return fn

    def b_ml(hd):
        def fn():
            dh = dml // ML_HEADS
            y_ref[0, :, ds + hd * dh:ds + (hd + 1) * dh] = _ml_back(hd, o_b, h_ref, ng_ref)
        return fn

    nchunk = tile // ML_CHUNK
    all_rms = [f"rms{r}" for r in range(nrb)]
    mxu_stream = [(f"u{r}", 128, [f"rms{r}"], f_u(rblocks[r])) for r in range(nrb)]
    mxu_stream += [(f"perm{r}", 64, [f"u{q}" for q in range(nrb)], f_perm(rblocks[r]))
                   for r in range(nrb)]
    mxu_stream += [(f"qk0_{r}", 128, [f"rms{r}"], f_qk(0, r)) for r in range(nrb)]
    for c in range(nblk):
        if c + 1 < nblk:
            mxu_stream += [(f"qk{c + 1}_{r}", 128, [f"rms{r}"], f_qk(c + 1, r))
                           for r in range(nrb)]
        mxu_stream += [(f"bu{c}_{r}", 150, [f"perm{r}", scans[c][1]], functools.partial(
            _s5_bu, c, rblocks[r], up_f, bcat_ref, bu_ref)) for r in range(nrb)]
        if c == 0:
            mxu_stream += [(f"v{r}", 128, [f"rms{r}"], f_v(rblocks[r])) for r in range(nrb)]
        if c == 1:
            mxu_stream += [(f"o{r}", 256, [f"rms{r}"], f_o(rblocks[r])) for r in range(nrb)]

    vpu_stream = [(f"rms{r}", 150, [], f_rms(rblocks[r])) for r in range(nrb)]
    vpu_stream.append(("gates", 300, all_rms, f_gates))
    for j in range(nblk):
        vpu_stream.extend(scans[j][0])
        for part in range(nrb):
            vpu_stream.append((f"conv{j}_{part}", 275, [f"qk{j}_{part}"], functools.partial(
                _ml_conv, j, part, nrb, ncol, cw_ref, cb_ref, xc_ref, q_f, k_f)))
    _run_interleaved([mxu_stream, vpu_stream])

    chunk_stream = [(f"chunk{c}", 1500, [], functools.partial(
        _ml_chunk, c, q_b, k_b, v_b, a_b, h_ref, caug_ref, m_ref)) for c in range(nchunk)]
    chunk_stream += [(f"mlback{hd}", 300, [], b_ml(hd)) for hd in range(ML_HEADS)]
    out_stream = [(f"cproj{c}_{r}", 130, [], functools.partial(
        _s5_cproj, c, rblocks[r], xb_ref, ccat_ref, ys_ref))
        for c in range(nblk) for r in range(nrb)]
    out_stream += [(f"s5back{r}", 350, [], functools.partial(
        _s5_back, rblocks[r], ys_ref, up_b, d_ref, wglu_ref, bglu_ref, og_ref, zn_ref))
        for r in range(nrb)]
    out_stream += [(f"unperm{r}", 64, [], b_unperm(rblocks[r])) for r in range(nrb)]
    _run_interleaved([chunk_stream, out_stream])


def _mix_entry(*refs, **static):
    step = pl.program_id(0)
    for parity in range(2):
        pl.when(step % 2 == parity)(functools.partial(_mix_kernel, *refs, front=parity, **static))


def _layer_spec(arr, layer, block=None):
    block = tuple(arr.shape[1:]) if block is None else block
    nd = len(block)
    return pl.BlockSpec((None,) + block, lambda *_: (layer,) + (0,) * nd,
                        pipeline_mode=pl.Buffered(1))


def _mixer(x, layer, p):
    bsz, seq, dm = x.shape
    tile = MIX_TILE
    seg = tile // SUBLANES
    tiles_per_seq = seq // tile
    ntiles = bsz * tiles_per_seq
    ds = p["w_glu"].shape[2]
    dml = p["ml_norm_g"].shape[2]
    dh = dml // ML_HEADS
    nmain = ds + 4 * dml
    nst = p["ar"].shape[2]
    nblk = p["bcat"].shape[1]
    r = jnp.arange(tile)
    src = (r % SUBLANES) * seg + r // SUBLANES
    perm = (jnp.arange(tile)[None, :] == src[:, None]).astype(BF16)
    stacked = [p[k] for k in ("norm_mix_g", "w_in", "wg", "gb")]
    consts = [perm, perm.T]
    stacked2 = [p[k] for k in ("bcat", "ar", "ai", "asr", "asi", "ccat", "s5_d", "w_glu", "b_glu",
                               "s5_out_g", "conv_w", "conv_b", "ml_norm_g")]
    args = [x.reshape(ntiles, tile, dm)] + stacked + consts + stacked2
    in_specs = [pl.BlockSpec((1, tile, dm), lambda s: (jnp.minimum(s, ntiles - 1), 0, 0))]
    in_specs += [_layer_spec(a, layer, (dm, nmain) if a is p["w_in"] else None) for a in stacked]
    in_specs += [_const_spec(a.shape) for a in consts]
    in_specs += [_layer_spec(a, layer) for a in stacked2]
    y = pl.pallas_call(
        functools.partial(_mix_entry, seg=seg, tiles_per_seq=tiles_per_seq),
        grid=(ntiles + 1,),
        in_specs=in_specs,
        out_specs=pl.BlockSpec((1, tile, ds + dml), lambda s: (jnp.maximum(s - 1, 0), 0, 0)),
        out_shape=jax.ShapeDtypeStruct((ntiles, tile, ds + dml), BF16),
        scratch_shapes=[
            pltpu.VMEM((tile, dm), BF16),
            pltpu.VMEM((tile, ds), BF16),
            pltpu.VMEM((2, tile, ds), BF16),
            pltpu.VMEM((nblk, tile, 2 * nst // nblk), F32),
            pltpu.VMEM((nblk, tile, 2 * nst // nblk), F32),
            pltpu.VMEM((nblk, tile, 2 * nst // nblk), BF16),
            pltpu.VMEM((tile, ds), F32),
            pltpu.VMEM((tile, ds), BF16),
            pltpu.VMEM((2, nst), F32),
            pltpu.VMEM((2 * dml // LANES, tile + SUBLANES, LANES), F32),
            pltpu.VMEM((2, tile, dml), BF16),
            pltpu.VMEM((2, tile, dml), BF16),
            pltpu.VMEM((2, tile, dml), BF16),
            pltpu.VMEM((2, tile, dml), F32),
            pltpu.VMEM((tile, dml), F32),
            pltpu.VMEM((2, 3, tile // ML_CHUNK, 2 * ML_HEADS, ML_CHUNK), F32),
            pltpu.VMEM((ML_HEADS, dh, 2 * dh), F32),
            pltpu.VMEM((2 * ML_HEADS, LANES), F32),
        ],
        compiler_params=pltpu.CompilerParams(
            dimension_semantics=("arbitrary",), vmem_limit_bytes=VMEM_LIMIT),
        name="mixer",
    )(*args)
    return y.reshape(bsz, seq, ds + dml)


def _ffn_kernel(x_ref, y_ref, wo_ref, g_ref, w1_ref, w2_ref, fg_ref, o_ref, *, final):
    x1 = x_ref[...] + jnp.dot(y_ref[...], wo_ref[...], preferred_element_type=F32)
    hn = _rms(x1, g_ref[...]).astype(BF16)
    o_ref[...] = x1
    dff = w1_ref.shape[1]
    for c in range(dff // FFN_COL):
        cs = slice(c * FFN_COL, (c + 1) * FFN_COL)
        a = jnp.maximum(jnp.dot(hn, w1_ref[:, cs], preferred_element_type=F32), 0.0)
        o_ref[...] += jnp.dot((a * a).astype(BF16), w2_ref[cs, :], preferred_element_type=F32)
    if final:
        o_ref[...] = _rms(o_ref[...], fg_ref[...])


def _ffn_block(x2, y2, layer, p, final):
    n, dm = x2.shape
    tile = FFN_TILE
    stacked = [p[k] for k in ("w_out", "norm_ffn_g", "w_ff1", "w_ff2")]
    args = [x2, y2] + stacked + [p["final_norm_g"]]
    row = lambda w: pl.BlockSpec((tile, w), lambda i: (i, 0))
    in_specs = ([row(dm), row(y2.shape[1])] + [_layer_spec(a, layer) for a in stacked]
                + [_const_spec(p["final_norm_g"].shape)])
    return pl.pallas_call(
        functools.partial(_ffn_kernel, final=final),
        grid=(n // tile,),
        in_specs=in_specs,
        out_specs=row(dm),
        out_shape=jax.ShapeDtypeStruct((n, dm), F32),
        compiler_params=pltpu.CompilerParams(
            dimension_semantics=("arbitrary",), vmem_limit_bytes=VMEM_LIMIT),
        name="out_ffn",
    )(*args)


def kernel(x, norm_mix_g, w_in, s5_a_re, s5_a_im, s5_log_dt, s5_b_re, s5_b_im, s5_c_re, s5_c_im, s5_d, s5_w_glu, s5_b_glu, s5_out_g, ml_conv_w, ml_conv_b, ml_b_i, ml_b_f, ml_norm_g, w_out, norm_ffn_g, w_ff1, w_ff2, final_norm_g):
    bsz, seq, dm = x.shape
    depth = w_in.shape[0]
    ds = s5_w_glu.shape[2]
    dml = ml_norm_g.shape[1]
    nmain = ds + 4 * dml
    seg = MIX_TILE // SUBLANES
    bcat, ccat, ar, ai, asr, asi = jax.vmap(
        functools.partial(_s5_params, seg=seg))(s5_a_re, s5_a_im, s5_log_dt, s5_b_re, s5_b_im,
                                                s5_c_re, s5_c_im)
    row = lambda v: v.reshape(depth, 1, v.shape[-1])
    p = dict(
        norm_mix_g=row(norm_mix_g), w_in=w_in[:, :, :nmain].astype(BF16),
        wg=jnp.swapaxes(w_in[:, :, nmain:], 1, 2).astype(BF16),
        gb=jnp.concatenate([ml_b_i, ml_b_f], axis=1)[:, :, None],
        bcat=bcat, ar=ar, ai=ai, asr=asr, asi=asi, ccat=ccat, s5_d=row(s5_d.reshape(depth, ds)),
        w_glu=s5_w_glu.astype(BF16), b_glu=row(s5_b_glu), s5_out_g=row(s5_out_g),
        conv_w=ml_conv_w, conv_b=row(ml_conv_b), ml_norm_g=row(ml_norm_g),
        w_out=w_out.astype(BF16), norm_ffn_g=row(norm_ffn_g), w_ff1=w_ff1.astype(BF16),
        w_ff2=w_ff2.astype(BF16), final_norm_g=final_norm_g.reshape(1, dm))
    for l in range(depth):
        y = _mixer(x, l, p)
        x = _ffn_block(x.reshape(bsz * seq, dm), y.reshape(bsz * seq, y.shape[2]), l, p,
                       final=(l == depth - 1)).reshape(bsz, seq, dm)
    return x
```

```python
import functools
import math

import jax
import jax.numpy as jnp
from jax import lax
from jax.experimental import pallas as pl
from jax.experimental.pallas import tpu as pltpu

EPS = 1e-6
SSM_GROUP = 16
SSM_STATE = 64
ML_HEADS = 4
CONV_WIDTH = 4

SUBLANES = 8
LANES = 128
V7X_VMEM_BYTES = 64 * 1024 * 1024
VMEM_LIMIT = V7X_VMEM_BYTES - 8 * 1024 * 1024

MIX_TILE = 512
MIX_ROW_BLOCK = 128
ML_CHUNK = LANES
FFN_TILE = 512
FFN_COL = 1024
CAST_ROWS = 256
S5_BLOCK_GROUPS = 8
S5_SCAN_PIECE = 16

F32 = jnp.float32
BF16 = jnp.bfloat16


def _rms(x, g):
    ms = jnp.mean(x * x, axis=-1, keepdims=True)
    return x * lax.rsqrt(ms + EPS) * g


def _sigmoid(x):
    return 1.0 / (1.0 + jnp.exp(-x))


def _log_sigmoid(x):
    return jnp.minimum(x, 0.0) - jnp.log(1.0 + jnp.exp(-jnp.abs(x)))


def _gelu_tanh(x):
    c = math.sqrt(2.0 / math.pi)
    return x * (0.5 * (1.0 + jnp.tanh(c * (x + 0.044715 * (x * x * x)))))


def _const_spec(shape):
    nd = len(shape)
    return pl.BlockSpec(shape, lambda *_: (0,) * nd, pipeline_mode=pl.Buffered(1))


def _run_interleaved(streams):
    totals = [sum(p[1] for p in s) for s in streams]
    pos = [0] * len(streams)
    spent = [0.0] * len(streams)
    done = set()
    while any(pos[i] < len(s) for i, s in enumerate(streams)):
        ready = [i for i, s in enumerate(streams)
                 if pos[i] < len(s) and all(r in done for r in s[pos[i]][2])]
        assert ready, "interleaving deadlock"
        i = min(ready, key=lambda k: spent[k] / totals[k])
        name, cost, _, fn = streams[i][pos[i]]
        fn()
        done.add(name)
        spent[i] += cost
        pos[i] += 1


def _s5_bu(j, rows, up_ref, bcat_ref, bu_ref):
    hw2 = bcat_ref.shape[2]
    gw = up_ref.shape[1] // bcat_ref.shape[0]
    bu_ref[j, rows, :] = jnp.dot(
        up_ref[rows, j * gw:(j + 1) * gw], bcat_ref[j], preferred_element_type=F32)


def _s5_scan_pieces(j, ar_ref, ai_ref, asr_ref, asi_ref, bu_ref, xs_ref, xb_ref, carry_ref, seg):
    nst = ar_ref.shape[1]
    hw = nst // (nst // (S5_BLOCK_GROUPS * SSM_STATE))
    cr = slice(0, hw)
    ci = slice(hw, 2 * hw)
    blk = slice(j * hw, (j + 1) * hw)
    pair = 2 * SUBLANES
    st = {}

    def coeffs():
        return (jnp.broadcast_to(ar_ref[:, blk], (SUBLANES, hw)),
                jnp.broadcast_to(ai_ref[:, blk], (SUBLANES, hw)))

    def pass1(s0):
        def fn():
            ar, ai = coeffs()
            sr, si = st.get("s", (jnp.zeros((SUBLANES, hw), F32),) * 2)
            for s in range(s0, s0 + S5_SCAN_PIECE):
                rows = slice(s * SUBLANES, (s + 1) * SUBLANES)
                sr, si = (ar * sr - ai * si + bu_ref[j, rows, cr],
                          ar * si + ai * sr + bu_ref[j, rows, ci])
                xs_ref[j, rows, cr] = sr
                xs_ref[j, rows, ci] = si
            st["s"] = (sr, si)
        return fn

    def carry():
        fr, fi = st["s"]
        asr = asr_ref[:, blk]
        asi = asi_ref[:, blk]
        rows_r = [carry_ref[0:1, blk]]
        rows_i = [carry_ref[1:2, blk]]
        for k in range(SUBLANES):
            pr, pi = rows_r[-1], rows_i[-1]
            rows_r.append(asr * pr - asi * pi + fr[k:k + 1])
            rows_i.append(asr * pi + asi * pr + fi[k:k + 1])
        carry_ref[0:1, blk] = rows_r[SUBLANES]
        carry_ref[1:2, blk] = rows_i[SUBLANES]
        st["w"] = (jnp.concatenate(rows_r[:SUBLANES], axis=0),
                   jnp.concatenate(rows_i[:SUBLANES], axis=0))

    def pass2(s0):
        def fn():
            ar, ai = coeffs()
            wr, wi = st["w"]
            for s in range(s0, s0 + S5_SCAN_PIECE, 2):
                r0 = slice(s * SUBLANES, (s + 1) * SUBLANES)
                r1 = slice((s + 1) * SUBLANES, (s + 2) * SUBLANES)
                w1r, w1i = ar * wr - ai * wi, ar * wi + ai * wr
                wr, wi = ar * w1r - ai * w1i, ar * w1i + ai * w1r
                both = slice(s * SUBLANES, (s + 2) * SUBLANES)
                xb_ref[j, both, cr] = jnp.concatenate(
                    [xs_ref[j, r0, cr] + w1r, xs_ref[j, r1, cr] + wr], axis=0).astype(BF16)
                xb_ref[j, both, ci] = jnp.concatenate(
                    [xs_ref[j, r0, ci] + w1i, xs_ref[j, r1, ci] + wi], axis=0).astype(BF16)
            st["w"] = (wr, wi)
        return fn

    pieces = []
    prev = []
    n = seg // S5_SCAN_PIECE
    for q in range(n):
        name = f"p1_{j}_{q}"
        pieces.append((name, 150, prev, pass1(q * S5_SCAN_PIECE)))
        prev = [name]
    pieces.append((f"carry{j}", 60, prev, carry))
    prev = [f"carry{j}"]
    for q in range(n):
        name = f"p2_{j}_{q}"
        pieces.append((name, 150, prev, pass2(q * S5_SCAN_PIECE)))
        prev = [name]
    return pieces, f"p1_{j}_{n - 1}", f"p2_{j}_{n - 1}"


def _s5_cproj(j, rows, xb_ref, ccat_ref, ys_ref):
    hw2 = ccat_ref.shape[1]
    gw = ccat_ref.shape[2]
    ys_ref[rows, j * gw:(j + 1) * gw] = jnp.dot(xb_ref[j, rows, :],
                                                ccat_ref[j], preferred_element_type=F32)


def _s5_back(rows, ys_ref, up_ref, d_ref, wglu_ref, bglu_ref, og_ref, zn_ref):
    y = ys_ref[rows, :] + d_ref[...] * up_ref[rows, :].astype(F32)
    z = _gelu_tanh(y)
    gate = _sigmoid(jnp.dot(z.astype(BF16), wglu_ref[...], preferred_element_type=F32)
                    + bglu_ref[...])
    zn_ref[rows, :] = _rms(z * gate, og_ref[...]).astype(BF16)


def _s5_params(a_re, a_im, log_dt, b_re, b_im, c_re, c_im, seg):
    dt = jnp.exp(log_dt)[:, None]
    mag = jnp.exp(a_re * dt)
    ab_re = mag * jnp.cos(a_im * dt)
    ab_im = mag * jnp.sin(a_im * dt)
    den = jnp.square(a_re) + jnp.square(a_im)
    zr = ab_re - 1.0
    s_re = (zr * a_re + ab_im * a_im) / den
    s_im = (ab_im * a_re - zr * a_im) / den
    bb_re = s_re[..., None] * b_re - s_im[..., None] * b_im
    bb_im = s_re[..., None] * b_im + s_im[..., None] * b_re
    mag_s = jnp.exp(a_re * dt * seg)
    as_re = mag_s * jnp.cos(a_im * dt * seg)
    as_im = mag_s * jnp.sin(a_im * dt * seg)

    g, n, p = bb_re.shape
    gb = S5_BLOCK_GROUPS
    nblk = g // gb
    eye = jnp.eye(gb, dtype=F32)

    def in_blocks(bb):
        t = bb.reshape(nblk, gb, n, p)
        m = jnp.einsum('jgnp,gh->jgphn', t, eye)
        return m.reshape(nblk, gb * p, gb * n)

    def out_blocks(cc):
        t = cc.reshape(nblk, gb, p, n)
        m = jnp.einsum('jgpn,gh->jgnhp', t, eye)
        return m.reshape(nblk, gb * n, gb * p)

    bcat = jnp.concatenate([in_blocks(bb_re), in_blocks(bb_im)], axis=2).astype(BF16)
    ccat = jnp.concatenate([out_blocks(c_re), -out_blocks(c_im)], axis=1).astype(BF16)
    flat = lambda v: v.reshape(1, g * n)
    return bcat, ccat, flat(ab_re), flat(ab_im), flat(as_re), flat(as_im)


def _ml_conv(c, part, nparts, ncol, cw_ref, cb_ref, xc_ref, q_ref, k_ref):
    tile, dml = q_ref.shape
    dh = dml // ML_HEADS
    tail = SUBLANES
    nrow = tile // nparts
    r0 = part * nrow
    rows = slice(r0, r0 + nrow)
    for sl in range(ncol // LANES):
        slab = c * (ncol // LANES) + sl
        cols = slice(slab * LANES, (slab + 1) * LANES)
        acc = cb_ref[:, cols]
        for j in range(CONV_WIDTH):
            off = r0 + tail - (CONV_WIDTH - 1) + j
            acc = acc + cw_ref[j:j + 1, cols] * xc_ref[slab, pl.ds(off, nrow, stride=1), :]
        if part == nparts - 1:
            xc_ref[slab, 0:tail, :] = xc_ref[slab, tile:tile + tail, :]
        qk = acc * _sigmoid(acc)
        if cols.stop <= dml:
            q_ref[rows, cols] = qk.astype(BF16)
        else:
            k_ref[rows, cols.start - dml:cols.stop - dml] = (
                qk * (1.0 / math.sqrt(dh))).astype(BF16)


def _ml_gate_rows(gt, a_ref):
    tile = gt.shape[1]
    chunk = ML_CHUNK
    lf = _log_sigmoid(gt)
    lane = lax.broadcasted_iota(jnp.int32, gt.shape, 1) % chunk
    neg_inf = jnp.float32(-jnp.inf)
    bc = lf
    shift = 1
    while shift < chunk:
        bc = bc + jnp.where(lane >= shift, pltpu.roll(bc, shift, axis=1), 0.0)
        shift *= 2
    a8 = gt - pltpu.roll(bc, ML_HEADS, axis=0)
    cm = a8
    shift = 1
    while shift < chunk:
        cm = jnp.maximum(cm, jnp.where(lane >= shift, pltpu.roll(cm, shift, axis=1), neg_inf))
        shift *= 2
    for c in range(tile // chunk):
        cs = slice(c * chunk, (c + 1) * chunk)
        a_ref[0, c] = a8[:, cs]
        a_ref[1, c] = cm[:, cs]
        a_ref[2, c] = bc[:, cs]


def _ml_chunk(c, q_ref, k_ref, v_ref, a_ref, h_ref, caug_ref, m_ref):
    tile, dml = q_ref.shape
    dh = dml // ML_HEADS
    chunk = ML_CHUNK
    neg_inf = jnp.float32(-jnp.inf)
    row_i = lax.broadcasted_iota(jnp.int32, (chunk, chunk), 0)
    col_i = lax.broadcasted_iota(jnp.int32, (chunk, chunk), 1)
    tril = col_i <= row_i
    ones_blk = jnp.ones((chunk, dh), BF16)

    def col_bcast(row):
        return jnp.broadcast_to(row, (chunk, chunk)).T

    rows = slice(c * chunk, (c + 1) * chunk)
    a_t, cm_t, bc_t = a_ref[0, c], a_ref[1, c], a_ref[2, c]
    for hd in range(ML_HEADS):
        sl = slice(hd * dh, (hd + 1) * dh)
        qh = q_ref[rows, sl]
        kh = k_ref[rows, sl]
        vh = v_ref[rows, sl]
        a_r = a_t[hd:hd + 1, :]
        cm_r = cm_t[hd:hd + 1, :]
        bc_r = bc_t[ML_HEADS + hd:ML_HEADS + hd + 1, :]
        m_prev = m_ref[hd:hd + 1, :]
        m_col = jnp.maximum(col_bcast(cm_r), m_prev)
        d_in = jnp.exp(jnp.where(tril, a_r - m_col, neg_inf))
        d_st = jnp.exp(m_prev - m_col)
        e = jnp.exp(-(col_bcast(bc_r) + m_col))
        s = lax.dot_general(qh, kh, (((1,), (1,)), ((), ())),
                            preferred_element_type=F32) * d_in
        lhs = jnp.concatenate([s, qh.astype(F32) * d_st], axis=1).astype(BF16)
        vaug = jnp.concatenate([vh, ones_blk], axis=1)
        caug = caug_ref[hd]
        rhs = jnp.concatenate([vaug, caug.astype(BF16)], axis=0)
        na = jnp.dot(lhs, rhs, preferred_element_type=F32)
        h_ref[rows, sl] = na[:, :dh] / jnp.maximum(jnp.abs(na[:, dh:]), e)
        wk_r = d_in[chunk - 1:chunk, :]
        decay = d_st[chunk - 1:chunk, :]
        kwt = (kh.T.astype(F32) * wk_r).astype(BF16)
        upd = jnp.dot(kwt, vaug, preferred_element_type=F32)
        caug_ref[hd] = jnp.concatenate([decay, decay], axis=1) * caug + upd
        m_ref[hd:hd + 1, :] = (bc_r[:, chunk - 1:chunk]
                               + jnp.maximum(cm_r[:, chunk - 1:chunk], m_prev))


def _ml_back(hd, o_ref, h_ref, ng_ref):
    dh = o_ref.shape[1] // ML_HEADS
    sl = slice(hd * dh, (hd + 1) * dh)
    hv = _sigmoid(o_ref[:, sl]) * h_ref[:, sl]
    return _rms(hv, ng_ref[:, sl]).astype(BF16)


def _mix_kernel(x_ref, g_ref, win_ref, wg_ref, gb_ref,
                p_ref, pt_ref, bcat_ref, ar_ref, ai_ref, asr_ref, asi_ref, ccat_ref, d_ref,
                wglu_ref, bglu_ref, og_ref, cw_ref, cb_ref, ng_ref,
                y_ref,
                hb_ref, ub_ref, up2_ref, bu_ref, xs_ref, xb_ref, ys_ref, zn_ref, carry_ref,
                xc_ref, q2_ref, k2_ref, v2_ref, o2_ref, h_ref, a2_ref, caug_ref, m_ref,
                *, seg, tiles_per_seq, front):
    tile = hb_ref.shape[0]
    ds = up2_ref.shape[2]
    dml = q2_ref.shape[2]
    nblk = bcat_ref.shape[0]
    ncol = 2 * dml // nblk
    tail = SUBLANES
    step = pl.program_id(0)
    back = 1 - front

    @pl.when(step == 0)
    def _():
        up2_ref[...] = jnp.zeros_like(up2_ref)
        bu_ref[...] = jnp.zeros_like(bu_ref)
        q2_ref[...] = jnp.zeros_like(q2_ref)
        k2_ref[...] = jnp.zeros_like(k2_ref)
        v2_ref[...] = jnp.zeros_like(v2_ref)
        o2_ref[...] = jnp.zeros_like(o2_ref)
        a2_ref[...] = jnp.zeros_like(a2_ref)
        carry_ref[...] = jnp.zeros_like(carry_ref)
        caug_ref[...] = jnp.zeros_like(caug_ref)
        m_ref[...] = jnp.zeros_like(m_ref)

    @pl.when(step % tiles_per_seq == 0)
    def _():
        xc_ref[:, 0:tail, :] = jnp.zeros((xc_ref.shape[0], tail, LANES), F32)

    @pl.when(step % tiles_per_seq == 1 % tiles_per_seq)
    def _():
        carry_ref[...] = jnp.zeros_like(carry_ref)
        caug_ref[...] = jnp.zeros_like(caug_ref)
        m_ref[...] = jnp.zeros_like(m_ref)

    up_f, q_f, k_f, v_f, o_f, a_f = (r.at[front] for r in
                                     (up2_ref, q2_ref, k2_ref, v2_ref, o2_ref, a2_ref))
    up_b, q_b, k_b, v_b, o_b, a_b = (r.at[back] for r in
                                     (up2_ref, q2_ref, k2_ref, v2_ref, o2_ref, a2_ref))

    rb = MIX_ROW_BLOCK
    nrb = tile // rb
    rblocks = [slice(r * rb, (r + 1) * rb) for r in range(nrb)]

    def proj(rows, lo, hi):
        return jnp.dot(hb_ref[rows, :], win_ref[:, lo:hi], preferred_element_type=F32)

    def f_rms(rows):
        def fn():
            hb_ref[rows, :] = _rms(x_ref[0, rows, :], g_ref[...]).astype(BF16)
        return fn

    def f_u(rows):
        def fn():
            ub_ref[rows, :] = proj(rows, 0, ds).astype(BF16)
        return fn

    def f_perm(rows):
        def fn():
            up_f[rows, :] = jnp.dot(p_ref[rows, :], ub_ref[...],
                                    preferred_element_type=F32).astype(BF16)
        return fn

    def f_gates():
        gt = lax.dot_general(wg_ref[...], hb_ref[...], (((1,), (1,)), ((), ())),
                             preferred_element_type=F32) + gb_ref[...]
        _ml_gate_rows(gt, a_f)

    def f_qk(c, r):
        def fn():
            v = proj(rblocks[r], ds + c * ncol, ds + (c + 1) * ncol)
            for sl in range(ncol // LANES):
                xc_ref[c * (ncol // LANES) + sl, tail + r * rb:tail + (r + 1) * rb, :] = (
                    v[:, sl * LANES:(sl + 1) * LANES])
        return fn

    def f_v(rows):
        def fn():
            v_f[rows, :] = proj(rows, ds + 2 * dml, ds + 3 * dml).astype(BF16)
        return fn

    def f_o(rows):
        def fn():
            o_f[rows, :] = proj(rows, ds + 3 * dml, ds + 4 * dml)
        return fn

    scans = [_s5_scan_pieces(j, ar_ref, ai_ref, asr_ref, asi_ref, bu_ref, xs_ref, xb_ref,
                             carry_ref, seg) for j in range(nblk)]

    def b_unperm(rows):
        def fn():
            y_ref[0, rows, :ds] = jnp.dot(pt_ref[rows, :], zn_ref[...],
                                          preferred_element_type=F32).astype(BF16)
        return fn

    def b_ml(hd):
        def fn():
            dh = dml // ML_HEADS
            y_ref[0, :, ds + hd * dh:ds + (hd + 1) * dh] = _ml_back(hd, o_b, h_ref, ng_ref)
        return fn

    nchunk = tile // ML_CHUNK
    all_rms = [f"rms{r}" for r in range(nrb)]
    mxu_stream = [(f"u{r}", 128, [f"rms{r}"], f_u(rblocks[r])) for r in range(nrb)]
    mxu_stream += [(f"perm{r}", 64, [f"u{q}" for q in range(nrb)], f_perm(rblocks[r]))
                   for r in range(nrb)]
    mxu_stream += [(f"qk0_{r}", 128, [f"rms{r}"], f_qk(0, r)) for r in range(nrb)]
    for c in range(nblk):
        if c + 1 < nblk:
            mxu_stream += [(f"qk{c + 1}_{r}", 128, [f"rms{r}"], f_qk(c + 1, r))
                           for r in range(nrb)]
        mxu_stream += [(f"bu{c}_{r}", 150, [f"perm{r}", scans[c][1]], functools.partial(
            _s5_bu, c, rblocks[r], up_f, bcat_ref, bu_ref)) for r in range(nrb)]
        if c == 0:
            mxu_stream += [(f"v{r}", 128, [f"rms{r}"], f_v(rblocks[r])) for r in range(nrb)]
        if c == 1:
            mxu_stream += [(f"o{r}", 256, [f"rms{r}"], f_o(rblocks[r])) for r in range(nrb)]

    vpu_stream = [(f"rms{r}", 150, [], f_rms(rblocks[r])) for r in range(nrb)]
    vpu_stream.append(("gates", 300, all_rms, f_gates))
    for j in range(nblk):
        vpu_stream.extend(scans[j][0])
        for part in range(nrb):
            vpu_stream.append((f"conv{j}_{part}", 275, [f"qk{j}_{part}"], functools.partial(
                _ml_conv, j, part, nrb, ncol, cw_ref, cb_ref, xc_ref, q_f, k_f)))
    _run_interleaved([mxu_stream, vpu_stream])

    chunk_stream = [(f"chunk{c}", 1500, [], functools.partial(
        _ml_chunk, c, q_b, k_b, v_b, a_b, h_ref, caug_ref, m_ref)) for c in range(nchunk)]
    chunk_stream += [(f"mlback{hd}", 300, [], b_ml(hd)) for hd in range(ML_HEADS)]
    out_stream = [(f"cproj{c}_{r}", 130, [], functools.partial(
        _s5_cproj, c, rblocks[r], xb_ref, ccat_ref, ys_ref))
        for c in range(nblk) for r in range(nrb)]
    out_stream += [(f"s5back{r}", 350, [], functools.partial(
        _s5_back, rblocks[r], ys_ref, up_b, d_ref, wglu_ref, bglu_ref, og_ref, zn_ref))
        for r in range(nrb)]
    out_stream += [(f"unperm{r}", 64, [], b_unperm(rblocks[r])) for r in range(nrb)]
    _run_interleaved([chunk_stream, out_stream])


def _mix_entry(*refs, **static):
    step = pl.program_id(0)
    for parity in range(2):
        pl.when(step % 2 == parity)(functools.partial(_mix_kernel, *refs, front=parity, **static))


def _cast_kernel(w_ref, o_ref):
    o_ref[...] = w_ref[...].astype(BF16)


def _cast_columns(w, ncols):
    depth, rows, _ = w.shape
    spec = pl.BlockSpec((None, CAST_ROWS, ncols), lambda l, i: (l, i, 0))
    return pl.pallas_call(
        _cast_kernel, grid=(depth, rows // CAST_ROWS), in_specs=[spec], out_specs=spec,
        out_shape=jax.ShapeDtypeStruct((depth, rows, ncols), BF16), name="cast_columns")(w)


def _layer_spec(arr, layer, block=None):
    block = tuple(arr.shape[1:]) if block is None else block
    nd = len(block)
    return pl.BlockSpec((None,) + block, lambda *_: (layer,) + (0,) * nd,
                        pipeline_mode=pl.Buffered(1))


def _mixer(x, layer, p):
    bsz, seq, dm = x.shape
    tile = MIX_TILE
    seg = tile // SUBLANES
    tiles_per_seq = seq // tile
    ntiles = bsz * tiles_per_seq
    ds = p["w_glu"].shape[2]
    dml = p["ml_norm_g"].shape[2]
    dh = dml // ML_HEADS
    nmain = ds + 4 * dml
    nst = p["ar"].shape[2]
    nblk = p["bcat"].shape[1]
    r = jnp.arange(tile)
    src = (r % SUBLANES) * seg + r // SUBLANES
    perm = (jnp.arange(tile)[None, :] == src[:, None]).astype(BF16)
    stacked = [p[k] for k in ("norm_mix_g", "w_in", "wg", "gb")]
    consts = [perm, perm.T]
    stacked2 = [p[k] for k in ("bcat", "ar", "ai", "asr", "asi", "ccat", "s5_d", "w_glu", "b_glu",
                               "s5_out_g", "conv_w", "conv_b", "ml_norm_g")]
    args = [x.reshape(ntiles, tile, dm)] + stacked + consts + stacked2
    in_specs = [pl.BlockSpec((1, tile, dm), lambda s: (jnp.minimum(s, ntiles - 1), 0, 0))]
    in_specs += [_layer_spec(a, layer, (dm, nmain) if a is p["w_in"] else None) for a in stacked]
    in_specs += [_const_spec(a.shape) for a in consts]
    in_specs += [_layer_spec(a, layer) for a in stacked2]
    y = pl.pallas_call(
        functools.partial(_mix_entry, seg=seg, tiles_per_seq=tiles_per_seq),
        grid=(ntiles + 1,),
        in_specs=in_specs,
        out_specs=pl.BlockSpec((1, tile, ds + dml), lambda s: (jnp.maximum(s - 1, 0), 0, 0)),
        out_shape=jax.ShapeDtypeStruct((ntiles, tile, ds + dml), BF16),
        scratch_shapes=[
            pltpu.VMEM((tile, dm), BF16),
            pltpu.VMEM((tile, ds), BF16),
            pltpu.VMEM((2, tile, ds), BF16),
            pltpu.VMEM((nblk, tile, 2 * nst // nblk), F32),
            pltpu.VMEM((nblk, tile, 2 * nst // nblk), F32),
            pltpu.VMEM((nblk, tile, 2 * nst // nblk), BF16),
            pltpu.VMEM((tile, ds), F32),
            pltpu.VMEM((tile, ds), BF16),
            pltpu.VMEM((2, nst), F32),
            pltpu.VMEM((2 * dml // LANES, tile + SUBLANES, LANES), F32),
            pltpu.VMEM((2, tile, dml), BF16),
            pltpu.VMEM((2, tile, dml), BF16),
            pltpu.VMEM((2, tile, dml), BF16),
            pltpu.VMEM((2, tile, dml), F32),
            pltpu.VMEM((tile, dml), F32),
            pltpu.VMEM((2, 3, tile // ML_CHUNK, 2 * ML_HEADS, ML_CHUNK), F32),
            pltpu.VMEM((ML_HEADS, dh, 2 * dh), F32),
            pltpu.VMEM((2 * ML_HEADS, LANES), F32),
        ],
        compiler_params=pltpu.CompilerParams(
            dimension_semantics=("arbitrary",), vmem_limit_bytes=VMEM_LIMIT),
        name="mixer",
    )(*args)
    return y.reshape(bsz, seq, ds + dml)


def _ffn_kernel(x_ref, y_ref, wo_ref, g_ref, w1_ref, w2_ref, fg_ref, o_ref, *, final):
    x1 = x_ref[...] + jnp.dot(y_ref[...], wo_ref[...], preferred_element_type=F32)
    hn = _rms(x1, g_ref[...]).astype(BF16)
    o_ref[...] = x1
    dff = w1_ref.shape[1]
    for c in range(dff // FFN_COL):
        cs = slice(c * FFN_COL, (c + 1) * FFN_COL)
        a = jnp.maximum(jnp.dot(hn, w1_ref[:, cs], preferred_element_type=F32), 0.0)
        o_ref[...] += jnp.dot((a * a).astype(BF16), w2_ref[cs, :], preferred_element_type=F32)
    if final:
        o_ref[...] = _rms(o_ref[...], fg_ref[...])


def _ffn_block(x2, y2, layer, p, final):
    n, dm = x2.shape
    tile = FFN_TILE
    stacked = [p[k] for k in ("w_out", "norm_ffn_g", "w_ff1", "w_ff2")]
    args = [x2, y2] + stacked + [p["final_norm_g"]]
    row = lambda w: pl.BlockSpec((tile, w), lambda i: (i, 0))
    in_specs = ([row(dm), row(y2.shape[1])] + [_layer_spec(a, layer) for a in stacked]
                + [_const_spec(p["final_norm_g"].shape)])
    return pl.pallas_call(
        functools.partial(_ffn_kernel, final=final),
        grid=(n // tile,),
        in_specs=in_specs,
        out_specs=row(dm),
        out_shape=jax.ShapeDtypeStruct((n, dm), F32),
        compiler_params=pltpu.CompilerParams(
            dimension_semantics=("arbitrary",), vmem_limit_bytes=VMEM_LIMIT),
        name="out_ffn",
    )(*args)


def kernel(x, norm_mix_g, w_in, s5_a_re, s5_a_im, s5_log_dt, s5_b_re, s5_b_im, s5_c_re, s5_c_im, s5_d, s5_w_glu, s5_b_glu, s5_out_g, ml_conv_w, ml_conv_b, ml_b_i, ml_b_f, ml_norm_g, w_out, norm_ffn_g, w_ff1, w_ff2, final_norm_g):
    bsz, seq, dm = x.shape
    depth = w_in.shape[0]
    ds = s5_w_glu.shape[2]
    dml = ml_norm_g.shape[1]
    nmain = ds + 4 * dml
    seg = MIX_TILE // SUBLANES
    bcat, ccat, ar, ai, asr, asi = jax.vmap(
        functools.partial(_s5_params, seg=seg))(s5_a_re, s5_a_im, s5_log_dt, s5_b_re, s5_b_im,
                                                s5_c_re, s5_c_im)
    row = lambda v: v.reshape(depth, 1, v.shape[-1])
    p = dict(
        norm_mix_g=row(norm_mix_g), w_in=_cast_columns(w_in, nmain),
        wg=jnp.swapaxes(w_in[:, :, nmain:], 1, 2).astype(BF16),
        gb=jnp.concatenate([ml_b_i, ml_b_f], axis=1)[:, :, None],
        bcat=bcat, ar=ar, ai=ai, asr=asr, asi=asi, ccat=ccat, s5_d=row(s5_d.reshape(depth, ds)),
        w_glu=s5_w_glu.astype(BF16), b_glu=row(s5_b_glu), s5_out_g=row(s5_out_g),
        conv_w=ml_conv_w, conv_b=row(ml_conv_b), ml_norm_g=row(ml_norm_g),
        w_out=w_out.astype(BF16), norm_ffn_g=row(norm_ffn_g), w_ff1=w_ff1.astype(BF16),
        w_ff2=w_ff2.astype(BF16), final_norm_g=final_norm_g.reshape(1, dm))
    for l in range(depth):
        y = _mixer(x, l, p)
        x = _ffn_block(x.reshape(bsz * seq, dm), y.reshape(bsz * seq, y.shape[2]), l, p,
                       final=(l == depth - 1)).reshape(bsz, seq, dm)
    return x
```

```python
import functools
import math

import jax
import jax.numpy as jnp
from jax import lax
from jax.experimental import pallas as pl
from jax.experimental.pallas import tpu as pltpu

EPS = 1e-6
SSM_GROUP = 16
SSM_STATE = 64
ML_HEADS = 4
CONV_WIDTH = 4

SUBLANES = 8
LANES = 128
V7X_VMEM_BYTES = 64 * 1024 * 1024
VMEM_LIMIT = V7X_VMEM_BYTES - 8 * 1024 * 1024

MIX_TILE = 512
MIX_ROW_BLOCK = 128
ML_CHUNK = LANES
FFN_TILE = 512
FFN_COL = 1024
S5_BLOCK_GROUPS = 8
S5_SCAN_PIECE = 16

F32 = jnp.float32
BF16 = jnp.bfloat16


def _rms(x, g):
    ms = jnp.mean(x * x, axis=-1, keepdims=True)
    return x * lax.rsqrt(ms + EPS) * g


LOG2E = math.log2(math.e)


def _sigmoid(x):
    return 1.0 / (1.0 + jnp.exp2(x * (-LOG2E)))


def _log_sigmoid(x):
    return jnp.minimum(x, 0.0) - jnp.log(1.0 + jnp.exp(-jnp.abs(x)))


def _gelu_tanh(x):
    k = 2.0 * math.sqrt(2.0 / math.pi) * LOG2E
    return x / (1.0 + jnp.exp2(x * (-k - (0.044715 * k) * (x * x))))


def _const_spec(shape):
    nd = len(shape)
    return pl.BlockSpec(shape, lambda *_: (0,) * nd, pipeline_mode=pl.Buffered(1))


def _run_interleaved(streams):
    totals = [sum(p[1] for p in s) for s in streams]
    pos = [0] * len(streams)
    spent = [0.0] * len(streams)
    done = set()
    while any(pos[i] < len(s) for i, s in enumerate(streams)):
        ready = [i for i, s in enumerate(streams)
                 if pos[i] < len(s) and all(r in done for r in s[pos[i]][2])]
        assert ready, "interleaving deadlock"
        i = min(ready, key=lambda k: spent[k] / totals[k])
        name, cost, _, fn = streams[i][pos[i]]
        fn()
        done.add(name)
        spent[i] += cost
        pos[i] += 1


def _s5_bu(j, rows, up_ref, bcat_ref, bu_ref):
    hw2 = bcat_ref.shape[2]
    gw = up_ref.shape[1] // bcat_ref.shape[0]
    bu_ref[j, rows, :] = jnp.dot(
        up_ref[rows, j * gw:(j + 1) * gw], bcat_ref[j], preferred_element_type=F32)


def _s5_scan_pieces(j, ar_ref, ai_ref, asr_ref, asi_ref, bu_ref, xs_ref, xb_ref, carry_ref, seg):
    nst = ar_ref.shape[1]
    hw = nst // (nst // (S5_BLOCK_GROUPS * SSM_STATE))
    cr = slice(0, hw)
    ci = slice(hw, 2 * hw)
    blk = slice(j * hw, (j + 1) * hw)
    pair = 2 * SUBLANES
    st = {}

    def coeffs():
        return (jnp.broadcast_to(ar_ref[:, blk], (SUBLANES, hw)),
                jnp.broadcast_to(ai_ref[:, blk], (SUBLANES, hw)))

    def pass1(s0):
        def fn():
            ar, ai = coeffs()
            sr, si = st.get("s", (jnp.zeros((SUBLANES, hw), F32),) * 2)
            for s in range(s0, s0 + S5_SCAN_PIECE):
                rows = slice(s * SUBLANES, (s + 1) * SUBLANES)
                sr, si = (ar * sr - ai * si + bu_ref[j, rows, cr],
                          ar * si + ai * sr + bu_ref[j, rows, ci])
                xs_ref[j, rows, cr] = sr
                xs_ref[j, rows, ci] = si
            st["s"] = (sr, si)
        return fn

    def carry():
        fr, fi = st["s"]
        asr = asr_ref[:, blk]
        asi = asi_ref[:, blk]
        rows_r = [carry_ref[0:1, blk]]
        rows_i = [carry_ref[1:2, blk]]
        for k in range(SUBLANES):
            pr, pi = rows_r[-1], rows_i[-1]
            rows_r.append(asr * pr - asi * pi + fr[k:k + 1])
            rows_i.append(asr * pi + asi * pr + fi[k:k + 1])
        carry_ref[0:1, blk] = rows_r[SUBLANES]
        carry_ref[1:2, blk] = rows_i[SUBLANES]
        st["w"] = (jnp.concatenate(rows_r[:SUBLANES], axis=0),
                   jnp.concatenate(rows_i[:SUBLANES], axis=0))

    def pass2(s0):
        def fn():
            ar, ai = coeffs()
            wr, wi = st["w"]
            for s in range(s0, s0 + S5_SCAN_PIECE, 2):
                r0 = slice(s * SUBLANES, (s + 1) * SUBLANES)
                r1 = slice((s + 1) * SUBLANES, (s + 2) * SUBLANES)
                w1r, w1i = ar * wr - ai * wi, ar * wi + ai * wr
                wr, wi = ar * w1r - ai * w1i, ar * w1i + ai * w1r
                both = slice(s * SUBLANES, (s + 2) * SUBLANES)
                xb_ref[j, both, cr] = jnp.concatenate(
                    [xs_ref[j, r0, cr] + w1r, xs_ref[j, r1, cr] + wr], axis=0).astype(BF16)
                xb_ref[j, both, ci] = jnp.concatenate(
                    [xs_ref[j, r0, ci] + w1i, xs_ref[j, r1, ci] + wi], axis=0).astype(BF16)
            st["w"] = (wr, wi)
        return fn

    pieces = []
    prev = []
    n = seg // S5_SCAN_PIECE
    for q in range(n):
        name = f"p1_{j}_{q}"
        pieces.append((name, 150, prev, pass1(q * S5_SCAN_PIECE)))
        prev = [name]
    pieces.append((f"carry{j}", 60, prev, carry))
    prev = [f"carry{j}"]
    for q in range(n):
        name = f"p2_{j}_{q}"
        pieces.append((name, 150, prev, pass2(q * S5_SCAN_PIECE)))
        prev = [name]
    return pieces, f"p1_{j}_{n - 1}", f"p2_{j}_{n - 1}"


def _s5_cproj(j, rows, xb_ref, ccat_ref, ys_ref):
    hw2 = ccat_ref.shape[1]
    gw = ccat_ref.shape[2]
    ys_ref[rows, j * gw:(j + 1) * gw] = jnp.dot(xb_ref[j, rows, :],
                                                ccat_ref[j], preferred_element_type=F32)


def _s5_back(rows, ys_ref, up_ref, d_ref, wglu_ref, bglu_ref, og_ref, zn_ref):
    y = ys_ref[rows, :] + d_ref[...] * up_ref[rows, :].astype(F32)
    z = _gelu_tanh(y)
    gate = _sigmoid(jnp.dot(z.astype(BF16), wglu_ref[...], preferred_element_type=F32)
                    + bglu_ref[...])
    zn_ref[rows, :] = _rms(z * gate, og_ref[...]).astype(BF16)


def _s5_params(a_re, a_im, log_dt, b_re, b_im, c_re, c_im, seg):
    dt = jnp.exp(log_dt)[:, None]
    mag = jnp.exp(a_re * dt)
    ab_re = mag * jnp.cos(a_im * dt)
    ab_im = mag * jnp.sin(a_im * dt)
    den = jnp.square(a_re) + jnp.square(a_im)
    zr = ab_re - 1.0
    s_re = (zr * a_re + ab_im * a_im) / den
    s_im = (ab_im * a_re - zr * a_im) / den
    bb_re = s_re[..., None] * b_re - s_im[..., None] * b_im
    bb_im = s_re[..., None] * b_im + s_im[..., None] * b_re
    mag_s = jnp.exp(a_re * dt * seg)
    as_re = mag_s * jnp.cos(a_im * dt * seg)
    as_im = mag_s * jnp.sin(a_im * dt * seg)

    g, n, p = bb_re.shape
    gb = S5_BLOCK_GROUPS
    nblk = g // gb
    eye = jnp.eye(gb, dtype=F32)

    def in_blocks(bb):
        t = bb.reshape(nblk, gb, n, p)
        m = jnp.einsum('jgnp,gh->jgphn', t, eye)
        return m.reshape(nblk, gb * p, gb * n)

    def out_blocks(cc):
        t = cc.reshape(nblk, gb, p, n)
        m = jnp.einsum('jgpn,gh->jgnhp', t, eye)
        return m.reshape(nblk, gb * n, gb * p)

    bcat = jnp.concatenate([in_blocks(bb_re), in_blocks(bb_im)], axis=2).astype(BF16)
    ccat = jnp.concatenate([out_blocks(c_re), -out_blocks(c_im)], axis=1).astype(BF16)
    flat = lambda v: v.reshape(1, g * n)
    return bcat, ccat, flat(ab_re), flat(ab_im), flat(as_re), flat(as_im)


def _ml_conv(c, part, nparts, ncol, cw_ref, cb_ref, xc_ref, q_ref, k_ref):
    tile, dml = q_ref.shape
    dh = dml // ML_HEADS
    tail = SUBLANES
    nrow = tile // nparts
    r0 = part * nrow
    rows = slice(r0, r0 + nrow)
    for sl in range(ncol // LANES):
        slab = c * (ncol // LANES) + sl
        cols = slice(slab * LANES, (slab + 1) * LANES)
        acc = cb_ref[:, cols]
        for j in range(CONV_WIDTH):
            off = r0 + tail - (CONV_WIDTH - 1) + j
            acc = acc + cw_ref[j:j + 1, cols] * xc_ref[slab, pl.ds(off, nrow, stride=1), :]
        if part == nparts - 1:
            xc_ref[slab, 0:tail, :] = xc_ref[slab, tile:tile + tail, :]
        qk = acc * _sigmoid(acc)
        if cols.stop <= dml:
            q_ref[rows, cols] = qk.astype(BF16)
        else:
            k_ref[rows, cols.start - dml:cols.stop - dml] = (
                qk * (1.0 / math.sqrt(dh))).astype(BF16)


def _ml_gate_rows(gt, a_ref):
    tile = gt.shape[1]
    chunk = ML_CHUNK
    lf = _log_sigmoid(gt)
    lane = lax.broadcasted_iota(jnp.int32, gt.shape, 1) % chunk
    neg_inf = jnp.float32(-jnp.inf)
    bc = lf
    shift = 1
    while shift < chunk:
        bc = bc + jnp.where(lane >= shift, pltpu.roll(bc, shift, axis=1), 0.0)
        shift *= 2
    a8 = gt - pltpu.roll(bc, ML_HEADS, axis=0)
    cm = a8
    shift = 1
    while shift < chunk:
        cm = jnp.maximum(cm, jnp.where(lane >= shift, pltpu.roll(cm, shift, axis=1), neg_inf))
        shift *= 2
    for c in range(tile // chunk):
        cs = slice(c * chunk, (c + 1) * chunk)
        a_ref[0, c] = a8[:, cs]
        a_ref[1, c] = cm[:, cs]
        a_ref[2, c] = bc[:, cs]


def _ml_chunk(c, q_ref, k_ref, v_ref, a_ref, h_ref, caug_ref, m_ref):
    tile, dml = q_ref.shape
    dh = dml // ML_HEADS
    chunk = ML_CHUNK
    neg_inf = jnp.float32(-jnp.inf)
    row_i = lax.broadcasted_iota(jnp.int32, (chunk, chunk), 0)
    col_i = lax.broadcasted_iota(jnp.int32, (chunk, chunk), 1)
    tril = col_i <= row_i
    ones_blk = jnp.ones((chunk, dh), BF16)

    def col_bcast(row):
        return jnp.broadcast_to(row, (chunk, chunk)).T

    rows = slice(c * chunk, (c + 1) * chunk)
    a_t, cm_t, bc_t = a_ref[0, c], a_ref[1, c], a_ref[2, c]
    for hd in range(ML_HEADS):
        sl = slice(hd * dh, (hd + 1) * dh)
        qh = q_ref[rows, sl]
        kh = k_ref[rows, sl]
        vh = v_ref[rows, sl]
        a_r = a_t[hd:hd + 1, :]
        cm_r = cm_t[hd:hd + 1, :]
        bc_r = bc_t[ML_HEADS + hd:ML_HEADS + hd + 1, :]
        m_prev = m_ref[hd:hd + 1, :]
        m_col = jnp.maximum(col_bcast(cm_r), m_prev)
        d_in = jnp.exp(jnp.where(tril, a_r - m_col, neg_inf))
        d_st = jnp.exp(m_prev - m_col)
        e = jnp.exp(-(col_bcast(bc_r) + m_col))
        s = lax.dot_general(qh, kh, (((1,), (1,)), ((), ())),
                            preferred_element_type=F32) * d_in
        lhs = jnp.concatenate([s, qh.astype(F32) * d_st], axis=1).astype(BF16)
        vaug = jnp.concatenate([vh, ones_blk], axis=1)
        caug = caug_ref[hd]
        rhs = jnp.concatenate([vaug, caug.astype(BF16)], axis=0)
        na = jnp.dot(lhs, rhs, preferred_element_type=F32)
        h_ref[rows, sl] = na[:, :dh] / jnp.maximum(jnp.abs(na[:, dh:]), e)
        wk_r = d_in[chunk - 1:chunk, :]
        decay = d_st[chunk - 1:chunk, :]
        kwt = (kh.T.astype(F32) * wk_r).astype(BF16)
        upd = jnp.dot(kwt, vaug, preferred_element_type=F32)
        caug_ref[hd] = jnp.concatenate([decay, decay], axis=1) * caug + upd
        m_ref[hd:hd + 1, :] = (bc_r[:, chunk - 1:chunk]
                               + jnp.maximum(cm_r[:, chunk - 1:chunk], m_prev))


def _ml_back(hd, o_ref, h_ref, ng_ref):
    dh = o_ref.shape[1] // ML_HEADS
    sl = slice(hd * dh, (hd + 1) * dh)
    hv = _sigmoid(o_ref[:, sl]) * h_ref[:, sl]
    return _rms(hv, ng_ref[:, sl]).astype(BF16)


def _mix_kernel(x_ref, g_ref, win_ref, wg_ref, gb_ref,
                p_ref, pt_ref, bcat_ref, ar_ref, ai_ref, asr_ref, asi_ref, ccat_ref, d_ref,
                wglu_ref, bglu_ref, og_ref, cw_ref, cb_ref, ng_ref,
                y_ref,
                hb_ref, ub_ref, up2_ref, bu_ref, xs_ref, xb_ref, ys_ref, zn_ref, carry_ref,
                xc_ref, q2_ref, k2_ref, v2_ref, o2_ref, h_ref, a2_ref, caug_ref, m_ref,
                *, seg, tiles_per_seq, front):
    tile = hb_ref.shape[0]
    ds = up2_ref.shape[2]
    dml = q2_ref.shape[2]
    nblk = bcat_ref.shape[0]
    ncol = 2 * dml // nblk
    tail = SUBLANES
    step = pl.program_id(0)
    back = 1 - front

    @pl.when(step == 0)
    def _():
        up2_ref[...] = jnp.zeros_like(up2_ref)
        bu_ref[...] = jnp.zeros_like(bu_ref)
        q2_ref[...] = jnp.zeros_like(q2_ref)
        k2_ref[...] = jnp.zeros_like(k2_ref)
        v2_ref[...] = jnp.zeros_like(v2_ref)
        o2_ref[...] = jnp.zeros_like(o2_ref)
        a2_ref[...] = jnp.zeros_like(a2_ref)
        carry_ref[...] = jnp.zeros_like(carry_ref)
        caug_ref[...] = jnp.zeros_like(caug_ref)
        m_ref[...] = jnp.zeros_like(m_ref)

    @pl.when(step % tiles_per_seq == 0)
    def _():
        xc_ref[:, 0:tail, :] = jnp.zeros((xc_ref.shape[0], tail, LANES), F32)

    @pl.when(step % tiles_per_seq == 1 % tiles_per_seq)
    def _():
        carry_ref[...] = jnp.zeros_like(carry_ref)
        caug_ref[...] = jnp.zeros_like(caug_ref)
        m_ref[...] = jnp.zeros_like(m_ref)

    up_f, q_f, k_f, v_f, o_f, a_f = (r.at[front] for r in
                                     (up2_ref, q2_ref, k2_ref, v2_ref, o2_ref, a2_ref))
    up_b, q_b, k_b, v_b, o_b, a_b = (r.at[back] for r in
                                     (up2_ref, q2_ref, k2_ref, v2_ref, o2_ref, a2_ref))

    rb = MIX_ROW_BLOCK
    nrb = tile // rb
    rblocks = [slice(r * rb, (r + 1) * rb) for r in range(nrb)]

    def proj(rows, lo, hi):
        return jnp.dot(hb_ref[rows, :], win_ref[:, lo:hi], preferred_element_type=F32)

    def f_rms(rows):
        def fn():
            hb_ref[rows, :] = _rms(x_ref[0, rows, :], g_ref[...]).astype(BF16)
        return fn

    def f_u(rows):
        def fn():
            ub_ref[rows, :] = proj(rows, 0, ds).astype(BF16)
        return fn

    def f_perm(rows):
        def fn():
            up_f[rows, :] = jnp.dot(p_ref[rows, :], ub_ref[...],
                                    preferred_element_type=F32).astype(BF16)
        return fn

    def f_gates():
        gt = lax.dot_general(wg_ref[...], hb_ref[...], (((1,), (1,)), ((), ())),
                             preferred_element_type=F32) + gb_ref[...]
        _ml_gate_rows(gt, a_f)

    def f_qk(c, r):
        def fn():
            v = proj(rblocks[r], ds + c * ncol, ds + (c + 1) * ncol)
            for sl in range(ncol // LANES):
                xc_ref[c * (ncol // LANES) + sl, tail + r * rb:tail + (r + 1) * rb, :] = (
                    v[:, sl * LANES:(sl + 1) * LANES])
        return fn

    def f_v(rows):
        def fn():
            v_f[rows, :] = proj(rows, ds + 2 * dml, ds + 3 * dml).astype(BF16)
        return fn

    def f_o(rows):
        def fn():
            o_f[rows, :] = proj(rows, ds + 3 * dml, ds + 4 * dml)
        return fn

    scans = [_s5_scan_pieces(j, ar_ref, ai_ref, asr_ref, asi_ref, bu_ref, xs_ref, xb_ref,
                             carry_ref, seg) for j in range(nblk)]

    def b_unperm(rows):
        def fn():
            y_ref[0, rows, :ds] = jnp.dot(pt_ref[rows, :], zn_ref[...],
                                          preferred_element_type=F32).astype(BF16)
        return fn

    def b_ml(hd):
        def fn():
            dh = dml // ML_HEADS
            y_ref[0, :, ds + hd * dh:ds + (hd + 1) * dh] = _ml_back(hd, o_b, h_ref, ng_ref)
        return fn

    nchunk = tile // ML_CHUNK
    all_rms = [f"rms{r}" for r in range(nrb)]
    mxu_stream = [(f"u{r}", 128, [f"rms{r}"], f_u(rblocks[r])) for r in range(nrb)]
    mxu_stream += [(f"perm{r}", 64, [f"u{q}" for q in range(nrb)], f_perm(rblocks[r]))
                   for r in range(nrb)]
    mxu_stream += [(f"qk0_{r}", 128, [f"rms{r}"], f_qk(0, r)) for r in range(nrb)]
    for c in range(nblk):
        if c + 1 < nblk:
            mxu_stream += [(f"qk{c + 1}_{r}", 128, [f"rms{r}"], f_qk(c + 1, r))
                           for r in range(nrb)]
        mxu_stream += [(f"bu{c}_{r}", 150, [f"perm{r}", scans[c][1]], functools.partial(
            _s5_bu, c, rblocks[r], up_f, bcat_ref, bu_ref)) for r in range(nrb)]
        if c == 0:
            mxu_stream += [(f"v{r}", 128, [f"rms{r}"], f_v(rblocks[r])) for r in range(nrb)]
        if c == 1:
            mxu_stream += [(f"o{r}", 256, [f"rms{r}"], f_o(rblocks[r])) for r in range(nrb)]

    vpu_stream = [(f"rms{r}", 150, [], f_rms(rblocks[r])) for r in range(nrb)]
    vpu_stream.append(("gates", 300, all_rms, f_gates))
    for j in range(nblk):
        vpu_stream.extend(scans[j][0])
        for part in range(nrb):
            vpu_stream.append((f"conv{j}_{part}", 275, [f"qk{j}_{part}"], functools.partial(
                _ml_conv, j, part, nrb, ncol, cw_ref, cb_ref, xc_ref, q_f, k_f)))
    _run_interleaved([mxu_stream, vpu_stream])

    chunk_stream = [(f"chunk{c}", 1500, [], functools.partial(
        _ml_chunk, c, q_b, k_b, v_b, a_b, h_ref, caug_ref, m_ref)) for c in range(nchunk)]
    chunk_stream += [(f"mlback{hd}", 300, [], b_ml(hd)) for hd in range(ML_HEADS)]
    out_stream = [(f"cproj{c}_{r}", 130, [], functools.partial(
        _s5_cproj, c, rblocks[r], xb_ref, ccat_ref, ys_ref))
        for c in range(nblk) for r in range(nrb)]
    out_stream += [(f"s5back{r}", 350, [], functools.partial(
        _s5_back, rblocks[r], ys_ref, up_b, d_ref, wglu_ref, bglu_ref, og_ref, zn_ref))
        for r in range(nrb)]
    out_stream += [(f"unperm{r}", 64, [], b_unperm(rblocks[r])) for r in range(nrb)]
    _run_interleaved([chunk_stream, out_stream])


def _mix_entry(*refs, **static):
    step = pl.program_id(0)
    for parity in range(2):
        pl.when(step % 2 == parity)(functools.partial(_mix_kernel, *refs, front=parity, **static))


def _layer_spec(arr, layer, block=None):
    block = tuple(arr.shape[1:]) if block is None else block
    nd = len(block)
    return pl.BlockSpec((None,) + block, lambda *_: (layer,) + (0,) * nd,
                        pipeline_mode=pl.Buffered(1))


def _mixer(x, layer, p):
    bsz, seq, dm = x.shape
    tile = MIX_TILE
    seg = tile // SUBLANES
    tiles_per_seq = seq // tile
    ntiles = bsz * tiles_per_seq
    ds = p["w_glu"].shape[2]
    dml = p["ml_norm_g"].shape[2]
    dh = dml // ML_HEADS
    nmain = ds + 4 * dml
    nst = p["ar"].shape[2]
    nblk = p["bcat"].shape[1]
    r = jnp.arange(tile)
    src = (r % SUBLANES) * seg + r // SUBLANES
    perm = (jnp.arange(tile)[None, :] == src[:, None]).astype(BF16)
    stacked = [p[k] for k in ("norm_mix_g", "w_in", "wg", "gb")]
    consts = [perm, perm.T]
    stacked2 = [p[k] for k in ("bcat", "ar", "ai", "asr", "asi", "ccat", "s5_d", "w_glu", "b_glu",
                               "s5_out_g", "conv_w", "conv_b", "ml_norm_g")]
    args = [x.reshape(ntiles, tile, dm)] + stacked + consts + stacked2
    in_specs = [pl.BlockSpec((1, tile, dm), lambda s: (jnp.minimum(s, ntiles - 1), 0, 0))]
    in_specs += [_layer_spec(a, layer, (dm, nmain) if a is p["w_in"] else None) for a in stacked]
    in_specs += [_const_spec(a.shape) for a in consts]
    in_specs += [_layer_spec(a, layer) for a in stacked2]
    y = pl.pallas_call(
        functools.partial(_mix_entry, seg=seg, tiles_per_seq=tiles_per_seq),
        grid=(ntiles + 1,),
        in_specs=in_specs,
        out_specs=pl.BlockSpec((1, tile, ds + dml), lambda s: (jnp.maximum(s - 1, 0), 0, 0)),
        out_shape=jax.ShapeDtypeStruct((ntiles, tile, ds + dml), BF16),
        scratch_shapes=[
            pltpu.VMEM((tile, dm), BF16),
            pltpu.VMEM((tile, ds), BF16),
            pltpu.VMEM((2, tile, ds), BF16),
            pltpu.VMEM((nblk, tile, 2 * nst // nblk), F32),
            pltpu.VMEM((nblk, tile, 2 * nst // nblk), F32),
            pltpu.VMEM((nblk, tile, 2 * nst // nblk), BF16),
            pltpu.VMEM((tile, ds), F32),
            pltpu.VMEM((tile, ds), BF16),
            pltpu.VMEM((2, nst), F32),
            pltpu.VMEM((2 * dml // LANES, tile + SUBLANES, LANES), F32),
            pltpu.VMEM((2, tile, dml), BF16),
            pltpu.VMEM((2, tile, dml), BF16),
            pltpu.VMEM((2, tile, dml), BF16),
            pltpu.VMEM((2, tile, dml), F32),
            pltpu.VMEM((tile, dml), F32),
            pltpu.VMEM((2, 3, tile // ML_CHUNK, 2 * ML_HEADS, ML_CHUNK), F32),
            pltpu.VMEM((ML_HEADS, dh, 2 * dh), F32),
            pltpu.VMEM((2 * ML_HEADS, LANES), F32),
        ],
        compiler_params=pltpu.CompilerParams(
            dimension_semantics=("arbitrary",), vmem_limit_bytes=VMEM_LIMIT),
        name="mixer",
    )(*args)
    return y.reshape(bsz, seq, ds + dml)


def _ffn_kernel(x_ref, y_ref, wo_ref, g_ref, w1_ref, w2_ref, fg_ref, o_ref, *, final):
    x1 = x_ref[...] + jnp.dot(y_ref[...], wo_ref[...], preferred_element_type=F32)
    hn = _rms(x1, g_ref[...]).astype(BF16)
    o_ref[...] = x1
    dff = w1_ref.shape[1]
    for c in range(dff // FFN_COL):
        cs = slice(c * FFN_COL, (c + 1) * FFN_COL)
        a = jnp.maximum(jnp.dot(hn, w1_ref[:, cs], preferred_element_type=F32), 0.0)
        o_ref[...] += jnp.dot((a * a).astype(BF16), w2_ref[cs, :], preferred_element_type=F32)
    if final:
        o_ref[...] = _rms(o_ref[...], fg_ref[...])


def _ffn_block(x2, y2, layer, p, final):
    n, dm = x2.shape
    tile = FFN_TILE
    stacked = [p[k] for k in ("w_out", "norm_ffn_g", "w_ff1", "w_ff2")]
    args = [x2, y2] + stacked + [p["final_norm_g"]]
    row = lambda w: pl.BlockSpec((tile, w), lambda i: (i, 0))
    in_specs = ([row(dm), row(y2.shape[1])] + [_layer_spec(a, layer) for a in stacked]
                + [_const_spec(p["final_norm_g"].shape)])
    return pl.pallas_call(
        functools.partial(_ffn_kernel, final=final),
        grid=(n // tile,),
        in_specs=in_specs,
        out_specs=row(dm),
        out_shape=jax.ShapeDtypeStruct((n, dm), F32),
        compiler_params=pltpu.CompilerParams(
            dimension_semantics=("arbitrary",), vmem_limit_bytes=VMEM_LIMIT),
        name="out_ffn",
    )(*args)


def kernel(x, norm_mix_g, w_in, s5_a_re, s5_a_im, s5_log_dt, s5_b_re, s5_b_im, s5_c_re, s5_c_im, s5_d, s5_w_glu, s5_b_glu, s5_out_g, ml_conv_w, ml_conv_b, ml_b_i, ml_b_f, ml_norm_g, w_out, norm_ffn_g, w_ff1, w_ff2, final_norm_g):
    bsz, seq, dm = x.shape
    depth = w_in.shape[0]
    ds = s5_w_glu.shape[2]
    dml = ml_norm_g.shape[1]
    nmain = ds + 4 * dml
    seg = MIX_TILE // SUBLANES
    bcat, ccat, ar, ai, asr, asi = jax.vmap(
        functools.partial(_s5_params, seg=seg))(s5_a_re, s5_a_im, s5_log_dt, s5_b_re, s5_b_im,
                                                s5_c_re, s5_c_im)
    row = lambda v: v.reshape(depth, 1, v.shape[-1])
    p = dict(
        norm_mix_g=row(norm_mix_g), w_in=w_in[:, :, :nmain].astype(BF16),
        wg=jnp.swapaxes(w_in[:, :, nmain:], 1, 2).astype(BF16),
        gb=jnp.concatenate([ml_b_i, ml_b_f], axis=1)[:, :, None],
        bcat=bcat, ar=ar, ai=ai, asr=asr, asi=asi, ccat=ccat, s5_d=row(s5_d.reshape(depth, ds)),
        w_glu=s5_w_glu.astype(BF16), b_glu=row(s5_b_glu), s5_out_g=row(s5_out_g),
        conv_w=ml_conv_w, conv_b=row(ml_conv_b), ml_norm_g=row(ml_norm_g),
        w_out=w_out.astype(BF16), norm_ffn_g=row(norm_ffn_g), w_ff1=w_ff1.astype(BF16),
        w_ff2=w_ff2.astype(BF16), final_norm_g=final_norm_g.reshape(1, dm))
    for l in range(depth):
        y = _mixer(x, l, p)
        x = _ffn_block(x.reshape(bsz * seq, dm), y.reshape(bsz * seq, y.shape[2]), l, p,
                       final=(l == depth - 1)).reshape(bsz, seq, dm)
    return x
```

```python
import functools
import math

import jax
import jax.numpy as jnp
from jax import lax
from jax.experimental import pallas as pl
from jax.experimental.pallas import tpu as pltpu

EPS = 1e-6
SSM_GROUP = 16
SSM_STATE = 64
ML_HEADS = 4
CONV_WIDTH = 4

SUBLANES = 8
LANES = 128
V7X_VMEM_BYTES = 64 * 1024 * 1024
VMEM_LIMIT = V7X_VMEM_BYTES - 8 * 1024 * 1024

MIX_TILE = 512
MIX_ROW_BLOCK = 128
ML_CHUNK = LANES
FFN_TILE = 512
FFN_COL = 1024
S5_BLOCK_GROUPS = 8
S5_SCAN_PIECE = 16

F32 = jnp.float32
BF16 = jnp.bfloat16


def _rms(x, g):
    ms = jnp.mean(x * x, axis=-1, keepdims=True)
    return x * lax.rsqrt(ms + EPS) * g


LOG2E = math.log2(math.e)


def _sigmoid(x):
    return 1.0 / (1.0 + jnp.exp2(x * (-LOG2E)))


def _log_sigmoid(x):
    return jnp.minimum(x, 0.0) - jnp.log(1.0 + jnp.exp(-jnp.abs(x)))


def _gelu_tanh(x):
    k = 2.0 * math.sqrt(2.0 / math.pi) * LOG2E
    return x / (1.0 + jnp.exp2(x * (-k - (0.044715 * k) * (x * x))))


def _const_spec(shape):
    nd = len(shape)
    return pl.BlockSpec(shape, lambda *_: (0,) * nd, pipeline_mode=pl.Buffered(1))


def _run_interleaved(streams):
    totals = [sum(p[1] for p in s) for s in streams]
    pos = [0] * len(streams)
    spent = [0.0] * len(streams)
    done = set()
    while any(pos[i] < len(s) for i, s in enumerate(streams)):
        ready = [i for i, s in enumerate(streams)
                 if pos[i] < len(s) and all(r in done for r in s[pos[i]][2])]
        assert ready, "interleaving deadlock"
        i = min(ready, key=lambda k: spent[k] / totals[k])
        name, cost, _, fn = streams[i][pos[i]]
        fn()
        done.add(name)
        spent[i] += cost
        pos[i] += 1


def _s5_bu(j, rows, up_ref, bcat_ref, bu_ref):
    hw2 = bcat_ref.shape[2]
    gw = up_ref.shape[1] // bcat_ref.shape[0]
    bu_ref[j, rows, :] = jnp.dot(
        up_ref[rows, j * gw:(j + 1) * gw], bcat_ref[j], preferred_element_type=F32)


def _s5_scan_pieces(j, ar_ref, ai_ref, asr_ref, asi_ref, bu_ref, xs_ref, xb_ref, carry_ref, seg):
    nst = ar_ref.shape[1]
    hw = nst // (nst // (S5_BLOCK_GROUPS * SSM_STATE))
    cr = slice(0, hw)
    ci = slice(hw, 2 * hw)
    blk = slice(j * hw, (j + 1) * hw)
    pair = 2 * SUBLANES
    st = {}

    def coeffs():
        return (jnp.broadcast_to(ar_ref[:, blk], (SUBLANES, hw)),
                jnp.broadcast_to(ai_ref[:, blk], (SUBLANES, hw)))

    def pass1(s0):
        def fn():
            ar, ai = coeffs()
            sr, si = st.get("s", (jnp.zeros((SUBLANES, hw), F32),) * 2)
            for s in range(s0, s0 + S5_SCAN_PIECE):
                rows = slice(s * SUBLANES, (s + 1) * SUBLANES)
                sr, si = (ar * sr - ai * si + bu_ref[j, rows, cr],
                          ar * si + ai * sr + bu_ref[j, rows, ci])
                xs_ref[j, rows, cr] = sr
                xs_ref[j, rows, ci] = si
            st["s"] = (sr, si)
        return fn

    def carry():
        fr, fi = st["s"]
        asr = asr_ref[:, blk]
        asi = asi_ref[:, blk]
        rows_r = [carry_ref[0:1, blk]]
        rows_i = [carry_ref[1:2, blk]]
        for k in range(SUBLANES):
            pr, pi = rows_r[-1], rows_i[-1]
            rows_r.append(asr * pr - asi * pi + fr[k:k + 1])
            rows_i.append(asr * pi + asi * pr + fi[k:k + 1])
        carry_ref[0:1, blk] = rows_r[SUBLANES]
        carry_ref[1:2, blk] = rows_i[SUBLANES]
        st["w"] = (jnp.concatenate(rows_r[:SUBLANES], axis=0),
                   jnp.concatenate(rows_i[:SUBLANES], axis=0))

    def pass2(s0):
        def fn():
            ar, ai = coeffs()
            wr, wi = st["w"]
            for s in range(s0, s0 + S5_SCAN_PIECE, 2):
                r0 = slice(s * SUBLANES, (s + 1) * SUBLANES)
                r1 = slice((s + 1) * SUBLANES, (s + 2) * SUBLANES)
                w1r, w1i = ar * wr - ai * wi, ar * wi + ai * wr
                wr, wi = ar * w1r - ai * w1i, ar * w1i + ai * w1r
                both = slice(s * SUBLANES, (s + 2) * SUBLANES)
                xb_ref[j, both, cr] = jnp.concatenate(
                    [xs_ref[j, r0, cr] + w1r, xs_ref[j, r1, cr] + wr], axis=0).astype(BF16)
                xb_ref[j, both, ci] = jnp.concatenate(
                    [xs_ref[j, r0, ci] + w1i, xs_ref[j, r1, ci] + wi], axis=0).astype(BF16)
            st["w"] = (wr, wi)
        return fn

    pieces = []
    prev = []
    n = seg // S5_SCAN_PIECE
    for q in range(n):
        name = f"p1_{j}_{q}"
        pieces.append((name, 150, prev, pass1(q * S5_SCAN_PIECE)))
        prev = [name]
    pieces.append((f"carry{j}", 60, prev, carry))
    prev = [f"carry{j}"]
    for q in range(n):
        name = f"p2_{j}_{q}"
        pieces.append((name, 150, prev, pass2(q * S5_SCAN_PIECE)))
        prev = [name]
    return pieces, f"p1_{j}_{n - 1}", f"p2_{j}_{n - 1}"


def _s5_cproj(j, rows, xb_ref, ccat_ref, ys_ref):
    hw2 = ccat_ref.shape[1]
    gw = ccat_ref.shape[2]
    ys_ref[rows, j * gw:(j + 1) * gw] = jnp.dot(xb_ref[j, rows, :],
                                                ccat_ref[j], preferred_element_type=F32)


def _s5_back(rows, ys_ref, up_ref, d_ref, wglu_ref, bglu_ref, og_ref, zn_ref):
    y = ys_ref[rows, :] + d_ref[...] * up_ref[rows, :].astype(F32)
    z = _gelu_tanh(y)
    gate = _sigmoid(jnp.dot(z.astype(BF16), wglu_ref[...], preferred_element_type=F32)
                    + bglu_ref[...])
    zn_ref[rows, :] = _rms(z * gate, og_ref[...]).astype(BF16)


def _s5_params(a_re, a_im, log_dt, b_re, b_im, c_re, c_im, seg):
    dt = jnp.exp(log_dt)[:, None]
    mag = jnp.exp(a_re * dt)
    ab_re = mag * jnp.cos(a_im * dt)
    ab_im = mag * jnp.sin(a_im * dt)
    den = jnp.square(a_re) + jnp.square(a_im)
    zr = ab_re - 1.0
    s_re = (zr * a_re + ab_im * a_im) / den
    s_im = (ab_im * a_re - zr * a_im) / den
    bb_re = s_re[..., None] * b_re - s_im[..., None] * b_im
    bb_im = s_re[..., None] * b_im + s_im[..., None] * b_re
    mag_s = jnp.exp(a_re * dt * seg)
    as_re = mag_s * jnp.cos(a_im * dt * seg)
    as_im = mag_s * jnp.sin(a_im * dt * seg)

    g, n, p = bb_re.shape
    gb = S5_BLOCK_GROUPS
    nblk = g // gb
    eye = jnp.eye(gb, dtype=F32)

    def in_blocks(bb):
        t = bb.reshape(nblk, gb, n, p)
        m = jnp.einsum('jgnp,gh->jgphn', t, eye)
        return m.reshape(nblk, gb * p, gb * n)

    def out_blocks(cc):
        t = cc.reshape(nblk, gb, p, n)
        m = jnp.einsum('jgpn,gh->jgnhp', t, eye)
        return m.reshape(nblk, gb * n, gb * p)

    bcat = jnp.concatenate([in_blocks(bb_re), in_blocks(bb_im)], axis=2).astype(BF16)
    ccat = jnp.concatenate([out_blocks(c_re), -out_blocks(c_im)], axis=1).astype(BF16)
    flat = lambda v: v.reshape(1, g * n)
    return bcat, ccat, flat(ab_re), flat(ab_im), flat(as_re), flat(as_im)


def _ml_conv(c, part, nparts, ncol, cw_ref, cb_ref, xc_ref, q_ref, k_ref):
    tile, dml = q_ref.shape
    dh = dml // ML_HEADS
    tail = SUBLANES
    nrow = tile // nparts
    r0 = part * nrow
    rows = slice(r0, r0 + nrow)
    for sl in range(ncol // LANES):
        slab = c * (ncol // LANES) + sl
        cols = slice(slab * LANES, (slab + 1) * LANES)
        acc = cb_ref[:, cols]
        for j in range(CONV_WIDTH):
            off = r0 + tail - (CONV_WIDTH - 1) + j
            acc = acc + cw_ref[j:j + 1, cols] * xc_ref[slab, pl.ds(off, nrow, stride=1), :]
        if part == nparts - 1:
            xc_ref[slab, 0:tail, :] = xc_ref[slab, tile:tile + tail, :]
        qk = acc * _sigmoid(acc)
        if cols.stop <= dml:
            q_ref[rows, cols] = qk.astype(BF16)
        else:
            k_ref[rows, cols.start - dml:cols.stop - dml] = (
                qk * (1.0 / math.sqrt(dh))).astype(BF16)


def _ml_gate_rows(gt, a_ref):
    tile = gt.shape[1]
    chunk = ML_CHUNK
    lf = _log_sigmoid(gt)
    lane = lax.broadcasted_iota(jnp.int32, gt.shape, 1) % chunk
    neg_inf = jnp.float32(-jnp.inf)
    bc = lf
    shift = 1
    while shift < chunk:
        bc = bc + jnp.where(lane >= shift, pltpu.roll(bc, shift, axis=1), 0.0)
        shift *= 2
    a8 = gt - pltpu.roll(bc, ML_HEADS, axis=0)
    cm = a8
    shift = 1
    while shift < chunk:
        cm = jnp.maximum(cm, jnp.where(lane >= shift, pltpu.roll(cm, shift, axis=1), neg_inf))
        shift *= 2
    for c in range(tile // chunk):
        cs = slice(c * chunk, (c + 1) * chunk)
        a_ref[0, c] = a8[:, cs]
        a_ref[1, c] = cm[:, cs]
        a_ref[2, c] = bc[:, cs]


def _ml_chunk(c, q_ref, k_ref, v_ref, a_ref, h_ref, caug_ref, m_ref):
    tile, dml = q_ref.shape
    dh = dml // ML_HEADS
    chunk = ML_CHUNK
    neg_inf = jnp.float32(-jnp.inf)
    row_i = lax.broadcasted_iota(jnp.int32, (chunk, chunk), 0)
    col_i = lax.broadcasted_iota(jnp.int32, (chunk, chunk), 1)
    tril = col_i <= row_i
    ones_blk = jnp.ones((chunk, dh), BF16)

    def col_bcast(row):
        return jnp.broadcast_to(row, (chunk, chunk)).T

    rows = slice(c * chunk, (c + 1) * chunk)
    a_t, cm_t, bc_t = a_ref[0, c], a_ref[1, c], a_ref[2, c]
    for hd in range(ML_HEADS):
        sl = slice(hd * dh, (hd + 1) * dh)
        qh = q_ref[rows, sl]
        kh = k_ref[rows, sl]
        vh = v_ref[rows, sl]
        a_r = a_t[hd:hd + 1, :]
        cm_r = cm_t[hd:hd + 1, :]
        bc_r = bc_t[ML_HEADS + hd:ML_HEADS + hd + 1, :]
        m_prev = m_ref[hd:hd + 1, :]
        m_r = jnp.maximum(cm_r, m_prev)
        m_col = col_bcast(m_r)
        d_in = jnp.exp(jnp.where(tril, a_r - m_col, neg_inf))
        d_st = col_bcast(jnp.exp(m_prev - m_r))
        e = col_bcast(jnp.exp(-(bc_r + m_r)))
        s = lax.dot_general(qh, kh, (((1,), (1,)), ((), ())),
                            preferred_element_type=F32) * d_in
        lhs = jnp.concatenate([s, qh.astype(F32) * d_st], axis=1).astype(BF16)
        vaug = jnp.concatenate([vh, ones_blk], axis=1)
        caug = caug_ref[hd]
        rhs = jnp.concatenate([vaug, caug.astype(BF16)], axis=0)
        na = jnp.dot(lhs, rhs, preferred_element_type=F32)
        h_ref[rows, sl] = na[:, :dh] / jnp.maximum(jnp.abs(na[:, dh:]), e)
        wk_r = d_in[chunk - 1:chunk, :]
        decay = d_st[chunk - 1:chunk, :]
        kwt = (kh.T.astype(F32) * wk_r).astype(BF16)
        upd = jnp.dot(kwt, vaug, preferred_element_type=F32)
        caug_ref[hd] = jnp.concatenate([decay, decay], axis=1) * caug + upd
        m_ref[hd:hd + 1, :] = (bc_r[:, chunk - 1:chunk]
                               + jnp.maximum(cm_r[:, chunk - 1:chunk], m_prev))


def _ml_back(hd, o_ref, h_ref, ng_ref):
    dh = o_ref.shape[1] // ML_HEADS
    sl = slice(hd * dh, (hd + 1) * dh)
    hv = _sigmoid(o_ref[:, sl]) * h_ref[:, sl]
    return _rms(hv, ng_ref[:, sl]).astype(BF16)


def _mix_kernel(x_ref, g_ref, win_ref, wg_ref, gb_ref,
                p_ref, pt_ref, bcat_ref, ar_ref, ai_ref, asr_ref, asi_ref, ccat_ref, d_ref,
                wglu_ref, bglu_ref, og_ref, cw_ref, cb_ref, ng_ref,
                y_ref,
                hb_ref, ub_ref, up2_ref, bu_ref, xs_ref, xb_ref, ys_ref, zn_ref, carry_ref,
                xc_ref, q2_ref, k2_ref, v2_ref, o2_ref, h_ref, a2_ref, caug_ref, m_ref,
                *, seg, tiles_per_seq, front):
    tile = hb_ref.shape[0]
    ds = up2_ref.shape[2]
    dml = q2_ref.shape[2]
    nblk = bcat_ref.shape[0]
    ncol = 2 * dml // nblk
    tail = SUBLANES
    step = pl.program_id(0)
    back = 1 - front

    @pl.when(step == 0)
    def _():
        up2_ref[...] = jnp.zeros_like(up2_ref)
        bu_ref[...] = jnp.zeros_like(bu_ref)
        q2_ref[...] = jnp.zeros_like(q2_ref)
        k2_ref[...] = jnp.zeros_like(k2_ref)
        v2_ref[...] = jnp.zeros_like(v2_ref)
        o2_ref[...] = jnp.zeros_like(o2_ref)
        a2_ref[...] = jnp.zeros_like(a2_ref)
        carry_ref[...] = jnp.zeros_like(carry_ref)
        caug_ref[...] = jnp.zeros_like(caug_ref)
        m_ref[...] = jnp.zeros_like(m_ref)

    @pl.when(step % tiles_per_seq == 0)
    def _():
        xc_ref[:, 0:tail, :] = jnp.zeros((xc_ref.shape[0], tail, LANES), F32)

    @pl.when(step % tiles_per_seq == 1 % tiles_per_seq)
    def _():
        carry_ref[...] = jnp.zeros_like(carry_ref)
        caug_ref[...] = jnp.zeros_like(caug_ref)
        m_ref[...] = jnp.zeros_like(m_ref)

    up_f, q_f, k_f, v_f, o_f, a_f = (r.at[front] for r in
                                     (up2_ref, q2_ref, k2_ref, v2_ref, o2_ref, a2_ref))
    up_b, q_b, k_b, v_b, o_b, a_b = (r.at[back] for r in
                                     (up2_ref, q2_ref, k2_ref, v2_ref, o2_ref, a2_ref))

    rb = MIX_ROW_BLOCK
    nrb = tile // rb
    rblocks = [slice(r * rb, (r + 1) * rb) for r in range(nrb)]

    def proj(rows, lo, hi):
        return jnp.dot(hb_ref[rows, :], win_ref[:, lo:hi], preferred_element_type=F32)

    def f_rms(rows):
        def fn():
            hb_ref[rows, :] = _rms(x_ref[0, rows, :], g_ref[...]).astype(BF16)
        return fn

    def f_u(rows):
        def fn():
            ub_ref[rows, :] = proj(rows, 0, ds).astype(BF16)
        return fn

    def f_perm(rows):
        def fn():
            up_f[rows, :] = jnp.dot(p_ref[rows, :], ub_ref[...],
                                    preferred_element_type=F32).astype(BF16)
        return fn

    def f_gates():
        gt = lax.dot_general(wg_ref[...], hb_ref[...], (((1,), (1,)), ((), ())),
                             preferred_element_type=F32) + gb_ref[...]
        _ml_gate_rows(gt, a_f)

    def f_qk(c, r):
        def fn():
            v = proj(rblocks[r], ds + c * ncol, ds + (c + 1) * ncol)
            for sl in range(ncol // LANES):
                xc_ref[c * (ncol // LANES) + sl, tail + r * rb:tail + (r + 1) * rb, :] = (
                    v[:, sl * LANES:(sl + 1) * LANES])
        return fn

    def f_v(rows):
        def fn():
            v_f[rows, :] = proj(rows, ds + 2 * dml, ds + 3 * dml).astype(BF16)
        return fn

    def f_o(rows):
        def fn():
            o_f[rows, :] = proj(rows, ds + 3 * dml, ds + 4 * dml)
        return fn

    scans = [_s5_scan_pieces(j, ar_ref, ai_ref, asr_ref, asi_ref, bu_ref, xs_ref, xb_ref,
                             carry_ref, seg) for j in range(nblk)]

    def b_unperm(rows):
        def fn():
            y_ref[0, rows, :ds] = jnp.dot(pt_ref[rows, :], zn_ref[...],
                                          preferred_element_type=F32).astype(BF16)
        return fn

    def b_ml(hd):
        def fn():
            dh = dml // ML_HEADS
            y_ref[0, :, ds + hd * dh:ds + (hd + 1) * dh] = _ml_back(hd, o_b, h_ref, ng_ref)
        return fn

    nchunk = tile // ML_CHUNK
    all_rms = [f"rms{r}" for r in range(nrb)]
    mxu_stream = [(f"u{r}", 128, [f"rms{r}"], f_u(rblocks[r])) for r in range(nrb)]
    mxu_stream += [(f"perm{r}", 64, [f"u{q}" for q in range(nrb)], f_perm(rblocks[r]))
                   for r in range(nrb)]
    mxu_stream += [(f"qk0_{r}", 128, [f"rms{r}"], f_qk(0, r)) for r in range(nrb)]
    for c in range(nblk):
        if c + 1 < nblk:
            mxu_stream += [(f"qk{c + 1}_{r}", 128, [f"rms{r}"], f_qk(c + 1, r))
                           for r in range(nrb)]
        mxu_stream += [(f"bu{c}_{r}", 150, [f"perm{r}", scans[c][1]], functools.partial(
            _s5_bu, c, rblocks[r], up_f, bcat_ref, bu_ref)) for r in range(nrb)]
        if c == 0:
            mxu_stream += [(f"v{r}", 128, [f"rms{r}"], f_v(rblocks[r])) for r in range(nrb)]
        if c == 1:
            mxu_stream += [(f"o{r}", 256, [f"rms{r}"], f_o(rblocks[r])) for r in range(nrb)]

    vpu_stream = [(f"rms{r}", 150, [], f_rms(rblocks[r])) for r in range(nrb)]
    vpu_stream.append(("gates", 300, all_rms, f_gates))
    for j in range(nblk):
        vpu_stream.extend(scans[j][0])
        for part in range(nrb):
            vpu_stream.append((f"conv{j}_{part}", 275, [f"qk{j}_{part}"], functools.partial(
                _ml_conv, j, part, nrb, ncol, cw_ref, cb_ref, xc_ref, q_f, k_f)))
    _run_interleaved([mxu_stream, vpu_stream])

    chunk_stream = [(f"chunk{c}", 1500, [], functools.partial(
        _ml_chunk, c, q_b, k_b, v_b, a_b, h_ref, caug_ref, m_ref)) for c in range(nchunk)]
    chunk_stream += [(f"mlback{hd}", 300, [], b_ml(hd)) for hd in range(ML_HEADS)]
    out_stream = [(f"cproj{c}_{r}", 130, [], functools.partial(
        _s5_cproj, c, rblocks[r], xb_ref, ccat_ref, ys_ref))
        for c in range(nblk) for r in range(nrb)]
    out_stream += [(f"s5back{r}", 350, [], functools.partial(
        _s5_back, rblocks[r], ys_ref, up_b, d_ref, wglu_ref, bglu_ref, og_ref, zn_ref))
        for r in range(nrb)]
    out_stream += [(f"unperm{r}", 64, [], b_unperm(rblocks[r])) for r in range(nrb)]
    _run_interleaved([chunk_stream, out_stream])


def _mix_entry(*refs, **static):
    step = pl.program_id(0)
    for parity in range(2):
        pl.when(step % 2 == parity)(functools.partial(_mix_kernel, *refs, front=parity, **static))


def _layer_spec(arr, layer, block=None):
    block = tuple(arr.shape[1:]) if block is None else block
    nd = len(block)
    return pl.BlockSpec((None,) + block, lambda *_: (layer,) + (0,) * nd,
                        pipeline_mode=pl.Buffered(1))


def _mixer(x, layer, p):
    bsz, seq, dm = x.shape
    tile = MIX_TILE
    seg = tile // SUBLANES
    tiles_per_seq = seq // tile
    ntiles = bsz * tiles_per_seq
    ds = p["w_glu"].shape[2]
    dml = p["ml_norm_g"].shape[2]
    dh = dml // ML_HEADS
    nmain = ds + 4 * dml
    nst = p["ar"].shape[2]
    nblk = p["bcat"].shape[1]
    r = jnp.arange(tile)
    src = (r % SUBLANES) * seg + r // SUBLANES
    perm = (jnp.arange(tile)[None, :] == src[:, None]).astype(BF16)
    stacked = [p[k] for k in ("norm_mix_g", "w_in", "wg", "gb")]
    consts = [perm, perm.T]
    stacked2 = [p[k] for k in ("bcat", "ar", "ai", "asr", "asi", "ccat", "s5_d", "w_glu", "b_glu",
                               "s5_out_g", "conv_w", "conv_b", "ml_norm_g")]
    args = [x.reshape(ntiles, tile, dm)] + stacked + consts + stacked2
    in_specs = [pl.BlockSpec((1, tile, dm), lambda s: (jnp.minimum(s, ntiles - 1), 0, 0))]
    in_specs += [_layer_spec(a, layer, (dm, nmain) if a is p["w_in"] else None) for a in stacked]
    in_specs += [_const_spec(a.shape) for a in consts]
    in_specs += [_layer_spec(a, layer) for a in stacked2]
    y = pl.pallas_call(
        functools.partial(_mix_entry, seg=seg, tiles_per_seq=tiles_per_seq),
        grid=(ntiles + 1,),
        in_specs=in_specs,
        out_specs=pl.BlockSpec((1, tile, ds + dml), lambda s: (jnp.maximum(s - 1, 0), 0, 0)),
        out_shape=jax.ShapeDtypeStruct((ntiles, tile, ds + dml), BF16),
        scratch_shapes=[
            pltpu.VMEM((tile, dm), BF16),
            pltpu.VMEM((tile, ds), BF16),
            pltpu.VMEM((2, tile, ds), BF16),
            pltpu.VMEM((nblk, tile, 2 * nst // nblk), F32),
            pltpu.VMEM((nblk, tile, 2 * nst // nblk), F32),
            pltpu.VMEM((nblk, tile, 2 * nst // nblk), BF16),
            pltpu.VMEM((tile, ds), F32),
            pltpu.VMEM((tile, ds), BF16),
            pltpu.VMEM((2, nst), F32),
            pltpu.VMEM((2 * dml // LANES, tile + SUBLANES, LANES), F32),
            pltpu.VMEM((2, tile, dml), BF16),
            pltpu.VMEM((2, tile, dml), BF16),
            pltpu.VMEM((2, tile, dml), BF16),
            pltpu.VMEM((2, tile, dml), F32),
            pltpu.VMEM((tile, dml), F32),
            pltpu.VMEM((2, 3, tile // ML_CHUNK, 2 * ML_HEADS, ML_CHUNK), F32),
            pltpu.VMEM((ML_HEADS, dh, 2 * dh), F32),
            pltpu.VMEM((2 * ML_HEADS, LANES), F32),
        ],
        compiler_params=pltpu.CompilerParams(
            dimension_semantics=("arbitrary",), vmem_limit_bytes=VMEM_LIMIT),
        name="mixer",
    )(*args)
    return y.reshape(bsz, seq, ds + dml)


def _ffn_kernel(x_ref, y_ref, wo_ref, g_ref, w1_ref, w2_ref, fg_ref, o_ref, *, final):
    x1 = x_ref[...] + jnp.dot(y_ref[...], wo_ref[...], preferred_element_type=F32)
    hn = _rms(x1, g_ref[...]).astype(BF16)
    o_ref[...] = x1
    dff = w1_ref.shape[1]
    for c in range(dff // FFN_COL):
        cs = slice(c * FFN_COL, (c + 1) * FFN_COL)
        a = jnp.maximum(jnp.dot(hn, w1_ref[:, cs], preferred_element_type=F32), 0.0)
        o_ref[...] += jnp.dot((a * a).astype(BF16), w2_ref[cs, :], preferred_element_type=F32)
    if final:
        o_ref[...] = _rms(o_ref[...], fg_ref[...])


def _ffn_block(x2, y2, layer, p, final):
    n, dm = x2.shape
    tile = FFN_TILE
    stacked = [p[k] for k in ("w_out", "norm_ffn_g", "w_ff1", "w_ff2")]
    args = [x2, y2] + stacked + [p["final_norm_g"]]
    row = lambda w: pl.BlockSpec((tile, w), lambda i: (i, 0))
    in_specs = ([row(dm), row(y2.shape[1])] + [_layer_spec(a, layer) for a in stacked]
                + [_const_spec(p["final_norm_g"].shape)])
    return pl.pallas_call(
        functools.partial(_ffn_kernel, final=final),
        grid=(n // tile,),
        in_specs=in_specs,
        out_specs=row(dm),
        out_shape=jax.ShapeDtypeStruct((n, dm), F32),
        compiler_params=pltpu.CompilerParams(
            dimension_semantics=("arbitrary",), vmem_limit_bytes=VMEM_LIMIT),
        name="out_ffn",
    )(*args)


def kernel(x, norm_mix_g, w_in, s5_a_re, s5_a_im, s5_log_dt, s5_b_re, s5_b_im, s5_c_re, s5_c_im, s5_d, s5_w_glu, s5_b_glu, s5_out_g, ml_conv_w, ml_conv_b, ml_b_i, ml_b_f, ml_norm_g, w_out, norm_ffn_g, w_ff1, w_ff2, final_norm_g):
    bsz, seq, dm = x.shape
    depth = w_in.shape[0]
    ds = s5_w_glu.shape[2]
    dml = ml_norm_g.shape[1]
    nmain = ds + 4 * dml
    seg = MIX_TILE // SUBLANES
    bcat, ccat, ar, ai, asr, asi = jax.vmap(
        functools.partial(_s5_params, seg=seg))(s5_a_re, s5_a_im, s5_log_dt, s5_b_re, s5_b_im,
                                                s5_c_re, s5_c_im)
    row = lambda v: v.reshape(depth, 1, v.shape[-1])
    p = dict(
        norm_mix_g=row(norm_mix_g), w_in=w_in[:, :, :nmain].astype(BF16),
        wg=jnp.swapaxes(w_in[:, :, nmain:], 1, 2).astype(BF16),
        gb=jnp.concatenate([ml_b_i, ml_b_f], axis=1)[:, :, None],
        bcat=bcat, ar=ar, ai=ai, asr=asr, asi=asi, ccat=ccat, s5_d=row(s5_d.reshape(depth, ds)),
        w_glu=s5_w_glu.astype(BF16), b_glu=row(s5_b_glu), s5_out_g=row(s5_out_g),
        conv_w=ml_conv_w, conv_b=row(ml_conv_b), ml_norm_g=row(ml_norm_g),
        w_out=w_out.astype(BF16), norm_ffn_g=row(norm_ffn_g), w_ff1=w_ff1.astype(BF16),
        w_ff2=w_ff2.astype(BF16), final_norm_g=final_norm_g.reshape(1, dm))
    for l in range(depth):
        y = _mixer(x, l, p)
        x = _ffn_block(x.reshape(bsz * seq, dm), y.reshape(bsz * seq, y.shape[2]), l, p,
                       final=(l == depth - 1)).reshape(bsz, seq, dm)
    return x
```

```python
import functools
import math

import jax
import jax.numpy as jnp
from jax import lax
from jax.experimental import pallas as pl
from jax.experimental.pallas import tpu as pltpu

EPS = 1e-6
SSM_GROUP = 16
SSM_STATE = 64
ML_HEADS = 4
CONV_WIDTH = 4

SUBLANES = 8
LANES = 128
V7X_VMEM_BYTES = 64 * 1024 * 1024
VMEM_LIMIT = V7X_VMEM_BYTES - 8 * 1024 * 1024

MIX_TILE = 512
MIX_ROW_BLOCK = 128
ML_CHUNK = LANES
FFN_TILE = 512
FFN_COL = 1024
S5_BLOCK_GROUPS = 8
S5_SCAN_PIECE = 16

F32 = jnp.float32
BF16 = jnp.bfloat16


def _rms(x, g):
    ms = jnp.mean(x * x, axis=-1, keepdims=True)
    return x * lax.rsqrt(ms + EPS) * g


LOG2E = math.log2(math.e)


def _sigmoid(x):
    return 1.0 / (1.0 + jnp.exp2(x * (-LOG2E)))


def _log_sigmoid(x):
    return jnp.minimum(x, 0.0) - jnp.log(1.0 + jnp.exp(-jnp.abs(x)))


def _gelu_tanh(x):
    k = 2.0 * math.sqrt(2.0 / math.pi) * LOG2E
    return x / (1.0 + jnp.exp2(x * (-k - (0.044715 * k) * (x * x))))


def _const_spec(shape):
    nd = len(shape)
    return pl.BlockSpec(shape, lambda *_: (0,) * nd, pipeline_mode=pl.Buffered(1))


def _run_interleaved(streams):
    totals = [sum(p[1] for p in s) for s in streams]
    pos = [0] * len(streams)
    spent = [0.0] * len(streams)
    done = set()
    while any(pos[i] < len(s) for i, s in enumerate(streams)):
        ready = [i for i, s in enumerate(streams)
                 if pos[i] < len(s) and all(r in done for r in s[pos[i]][2])]
        assert ready, "interleaving deadlock"
        i = min(ready, key=lambda k: spent[k] / totals[k])
        name, cost, _, fn = streams[i][pos[i]]
        fn()
        done.add(name)
        spent[i] += cost
        pos[i] += 1


def _s5_bu(j, rows, up_ref, bcat_ref, bu_ref):
    hw2 = bcat_ref.shape[2]
    gw = up_ref.shape[1] // bcat_ref.shape[0]
    bu_ref[j, rows, :] = jnp.dot(
        up_ref[rows, j * gw:(j + 1) * gw], bcat_ref[j], preferred_element_type=F32)


def _s5_scan_pieces(j, ar_ref, ai_ref, asr_ref, asi_ref, bu_ref, xs_ref, xb_ref, carry_ref, seg):
    nst = ar_ref.shape[1]
    hw = nst // (nst // (S5_BLOCK_GROUPS * SSM_STATE))
    cr = slice(0, hw)
    ci = slice(hw, 2 * hw)
    blk = slice(j * hw, (j + 1) * hw)
    pair = 2 * SUBLANES
    st = {}

    def coeffs():
        return (jnp.broadcast_to(ar_ref[:, blk], (SUBLANES, hw)),
                jnp.broadcast_to(ai_ref[:, blk], (SUBLANES, hw)))

    def pass1(s0):
        def fn():
            ar, ai = coeffs()
            sr, si = st.get("s", (jnp.zeros((SUBLANES, hw), F32),) * 2)
            for s in range(s0, s0 + S5_SCAN_PIECE):
                rows = slice(s * SUBLANES, (s + 1) * SUBLANES)
                sr, si = (ar * sr - ai * si + bu_ref[j, rows, cr],
                          ar * si + ai * sr + bu_ref[j, rows, ci])
                xs_ref[j, rows, cr] = sr
                xs_ref[j, rows, ci] = si
            st["s"] = (sr, si)
        return fn

    def carry():
        fr, fi = st["s"]
        asr = asr_ref[:, blk]
        asi = asi_ref[:, blk]
        rows_r = [carry_ref[0:1, blk]]
        rows_i = [carry_ref[1:2, blk]]
        for k in range(SUBLANES):
            pr, pi = rows_r[-1], rows_i[-1]
            rows_r.append(asr * pr - asi * pi + fr[k:k + 1])
            rows_i.append(asr * pi + asi * pr + fi[k:k + 1])
        carry_ref[0:1, blk] = rows_r[SUBLANES]
        carry_ref[1:2, blk] = rows_i[SUBLANES]
        st["w"] = (jnp.concatenate(rows_r[:SUBLANES], axis=0),
                   jnp.concatenate(rows_i[:SUBLANES], axis=0))

    def pass2(s0):
        def fn():
            ar, ai = coeffs()
            wr, wi = st["w"]
            for s in range(s0, s0 + S5_SCAN_PIECE, 2):
                r0 = slice(s * SUBLANES, (s + 1) * SUBLANES)
                r1 = slice((s + 1) * SUBLANES, (s + 2) * SUBLANES)
                w1r, w1i = ar * wr - ai * wi, ar * wi + ai * wr
                wr, wi = ar * w1r - ai * w1i, ar * w1i + ai * w1r
                both = slice(s * SUBLANES, (s + 2) * SUBLANES)
                xb_ref[j, both, cr] = jnp.concatenate(
                    [xs_ref[j, r0, cr] + w1r, xs_ref[j, r1, cr] + wr], axis=0).astype(BF16)
                xb_ref[j, both, ci] = jnp.concatenate(
                    [xs_ref[j, r0, ci] + w1i, xs_ref[j, r1, ci] + wi], axis=0).astype(BF16)
            st["w"] = (wr, wi)
        return fn

    pieces = []
    prev = []
    n = seg // S5_SCAN_PIECE
    for q in range(n):
        name = f"p1_{j}_{q}"
        pieces.append((name, 150, prev, pass1(q * S5_SCAN_PIECE)))
        prev = [name]
    pieces.append((f"carry{j}", 60, prev, carry))
    prev = [f"carry{j}"]
    for q in range(n):
        name = f"p2_{j}_{q}"
        pieces.append((name, 150, prev, pass2(q * S5_SCAN_PIECE)))
        prev = [name]
    return pieces, f"p1_{j}_{n - 1}", f"p2_{j}_{n - 1}"


def _s5_cproj(j, rows, xb_ref, ccat_ref, ys_ref):
    hw2 = ccat_ref.shape[1]
    gw = ccat_ref.shape[2]
    ys_ref[rows, j * gw:(j + 1) * gw] = jnp.dot(xb_ref[j, rows, :],
                                                ccat_ref[j], preferred_element_type=F32)


def _s5_back(rows, ys_ref, up_ref, d_ref, wglu_ref, bglu_ref, og_ref, zn_ref):
    y = ys_ref[rows, :] + d_ref[...] * up_ref[rows, :].astype(F32)
    z = _gelu_tanh(y)
    gate = _sigmoid(jnp.dot(z.astype(BF16), wglu_ref[...], preferred_element_type=F32)
                    + bglu_ref[...])
    zn_ref[rows, :] = _rms(z * gate, og_ref[...]).astype(BF16)


def _s5_params(a_re, a_im, log_dt, b_re, b_im, c_re, c_im, seg):
    dt = jnp.exp(log_dt)[:, None]
    mag = jnp.exp(a_re * dt)
    ab_re = mag * jnp.cos(a_im * dt)
    ab_im = mag * jnp.sin(a_im * dt)
    den = jnp.square(a_re) + jnp.square(a_im)
    zr = ab_re - 1.0
    s_re = (zr * a_re + ab_im * a_im) / den
    s_im = (ab_im * a_re - zr * a_im) / den
    bb_re = s_re[..., None] * b_re - s_im[..., None] * b_im
    bb_im = s_re[..., None] * b_im + s_im[..., None] * b_re
    mag_s = jnp.exp(a_re * dt * seg)
    as_re = mag_s * jnp.cos(a_im * dt * seg)
    as_im = mag_s * jnp.sin(a_im * dt * seg)

    g, n, p = bb_re.shape
    gb = S5_BLOCK_GROUPS
    nblk = g // gb
    eye = jnp.eye(gb, dtype=F32)

    def in_blocks(bb):
        t = bb.reshape(nblk, gb, n, p)
        m = jnp.einsum('jgnp,gh->jgphn', t, eye)
        return m.reshape(nblk, gb * p, gb * n)

    def out_blocks(cc):
        t = cc.reshape(nblk, gb, p, n)
        m = jnp.einsum('jgpn,gh->jgnhp', t, eye)
        return m.reshape(nblk, gb * n, gb * p)

    bcat = jnp.concatenate([in_blocks(bb_re), in_blocks(bb_im)], axis=2).astype(BF16)
    ccat = jnp.concatenate([out_blocks(c_re), -out_blocks(c_im)], axis=1).astype(BF16)
    flat = lambda v: v.reshape(1, g * n)
    return bcat, ccat, flat(ab_re), flat(ab_im), flat(as_re), flat(as_im)


def _ml_conv(c, part, nparts, ncol, cw_ref, cb_ref, xc_ref, q_ref, k_ref):
    tile, dml = q_ref.shape
    dh = dml // ML_HEADS
    tail = SUBLANES
    nrow = tile // nparts
    r0 = part * nrow
    rows = slice(r0, r0 + nrow)
    for sl in range(ncol // LANES):
        slab = c * (ncol // LANES) + sl
        cols = slice(slab * LANES, (slab + 1) * LANES)
        acc = cb_ref[:, cols]
        for j in range(CONV_WIDTH):
            off = r0 + tail - (CONV_WIDTH - 1) + j
            acc = acc + cw_ref[j:j + 1, cols] * xc_ref[slab, pl.ds(off, nrow, stride=1), :]
        if part == nparts - 1:
            xc_ref[slab, 0:tail, :] = xc_ref[slab, tile:tile + tail, :]
        qk = acc * _sigmoid(acc)
        if cols.stop <= dml:
            q_ref[rows, cols] = qk.astype(BF16)
        else:
            k_ref[rows, cols.start - dml:cols.stop - dml] = (
                qk * (1.0 / math.sqrt(dh))).astype(BF16)


def _ml_gate_rows(gt, a_ref):
    tile = gt.shape[1]
    chunk = ML_CHUNK
    lf = _log_sigmoid(gt)
    lane = lax.broadcasted_iota(jnp.int32, gt.shape, 1) % chunk
    neg_inf = jnp.float32(-jnp.inf)
    bc = lf
    shift = 1
    while shift < chunk:
        bc = bc + jnp.where(lane >= shift, pltpu.roll(bc, shift, axis=1), 0.0)
        shift *= 2
    a8 = gt - pltpu.roll(bc, ML_HEADS, axis=0)
    cm = a8
    shift = 1
    while shift < chunk:
        cm = jnp.maximum(cm, jnp.where(lane >= shift, pltpu.roll(cm, shift, axis=1), neg_inf))
        shift *= 2
    for c in range(tile // chunk):
        cs = slice(c * chunk, (c + 1) * chunk)
        a_ref[0, c] = a8[:, cs]
        a_ref[1, c] = cm[:, cs]
        a_ref[2, c] = bc[:, cs]


def _ml_chunk(c, q_ref, k_ref, v_ref, o_ref, a_ref, h_ref, caug_ref, m_ref):
    tile, dml = q_ref.shape
    dh = dml // ML_HEADS
    chunk = ML_CHUNK
    neg_inf = jnp.float32(-jnp.inf)
    row_i = lax.broadcasted_iota(jnp.int32, (chunk, chunk), 0)
    col_i = lax.broadcasted_iota(jnp.int32, (chunk, chunk), 1)
    tril = col_i <= row_i
    ones_blk = jnp.ones((chunk, dh), BF16)

    def col_bcast(row):
        return jnp.broadcast_to(row, (chunk, chunk)).T

    rows = slice(c * chunk, (c + 1) * chunk)
    a_t, cm_t, bc_t = a_ref[0, c], a_ref[1, c], a_ref[2, c]
    for hd in range(ML_HEADS):
        sl = slice(hd * dh, (hd + 1) * dh)
        qh = q_ref[rows, sl]
        kh = k_ref[rows, sl]
        vh = v_ref[rows, sl]
        a_r = a_t[hd:hd + 1, :]
        cm_r = cm_t[hd:hd + 1, :]
        bc_r = bc_t[ML_HEADS + hd:ML_HEADS + hd + 1, :]
        m_prev = m_ref[hd:hd + 1, :]
        m_r = jnp.maximum(cm_r, m_prev)
        m_col = col_bcast(m_r)
        d_in = jnp.exp(jnp.where(tril, a_r - m_col, neg_inf))
        d_st = col_bcast(jnp.exp(m_prev - m_r))
        e = col_bcast(jnp.exp(-(bc_r + m_r)))
        s = lax.dot_general(qh, kh, (((1,), (1,)), ((), ())),
                            preferred_element_type=F32) * d_in
        lhs = jnp.concatenate([s, qh.astype(F32) * d_st], axis=1).astype(BF16)
        vaug = jnp.concatenate([vh, ones_blk], axis=1)
        caug = caug_ref[hd]
        rhs = jnp.concatenate([vaug, caug.astype(BF16)], axis=0)
        na = jnp.dot(lhs, rhs, preferred_element_type=F32)
        h_ref[rows, sl] = na[:, :dh] / (jnp.maximum(jnp.abs(na[:, dh:]), e) * o_ref[rows, sl])
        wk_r = d_in[chunk - 1:chunk, :]
        decay = d_st[chunk - 1:chunk, :]
        kwt = (kh.T.astype(F32) * wk_r).astype(BF16)
        upd = jnp.dot(kwt, vaug, preferred_element_type=F32)
        caug_ref[hd] = jnp.concatenate([decay, decay], axis=1) * caug + upd
        m_ref[hd:hd + 1, :] = (bc_r[:, chunk - 1:chunk]
                               + jnp.maximum(cm_r[:, chunk - 1:chunk], m_prev))


def _ml_back(hd, o_ref, h_ref, ng_ref):
    dh = o_ref.shape[1] // ML_HEADS
    sl = slice(hd * dh, (hd + 1) * dh)
    del o_ref
    return _rms(h_ref[:, sl], ng_ref[:, sl]).astype(BF16)


def _mix_kernel(x_ref, g_ref, win_ref, wg_ref, gb_ref,
                p_ref, pt_ref, bcat_ref, ar_ref, ai_ref, asr_ref, asi_ref, ccat_ref, d_ref,
                wglu_ref, bglu_ref, og_ref, cw_ref, cb_ref, ng_ref,
                y_ref,
                hb_ref, ub_ref, up2_ref, bu_ref, xs_ref, xb_ref, ys_ref, zn_ref, carry_ref,
                xc_ref, q2_ref, k2_ref, v2_ref, o2_ref, h_ref, a2_ref, caug_ref, m_ref,
                *, seg, tiles_per_seq, front):
    tile = hb_ref.shape[0]
    ds = up2_ref.shape[2]
    dml = q2_ref.shape[2]
    nblk = bcat_ref.shape[0]
    ncol = 2 * dml // nblk
    tail = SUBLANES
    step = pl.program_id(0)
    back = 1 - front

    @pl.when(step == 0)
    def _():
        up2_ref[...] = jnp.zeros_like(up2_ref)
        bu_ref[...] = jnp.zeros_like(bu_ref)
        q2_ref[...] = jnp.zeros_like(q2_ref)
        k2_ref[...] = jnp.zeros_like(k2_ref)
        v2_ref[...] = jnp.zeros_like(v2_ref)
        o2_ref[...] = jnp.ones_like(o2_ref)
        a2_ref[...] = jnp.zeros_like(a2_ref)
        carry_ref[...] = jnp.zeros_like(carry_ref)
        caug_ref[...] = jnp.zeros_like(caug_ref)
        m_ref[...] = jnp.zeros_like(m_ref)

    @pl.when(step % tiles_per_seq == 0)
    def _():
        xc_ref[:, 0:tail, :] = jnp.zeros((xc_ref.shape[0], tail, LANES), F32)

    @pl.when(step % tiles_per_seq == 1 % tiles_per_seq)
    def _():
        carry_ref[...] = jnp.zeros_like(carry_ref)
        caug_ref[...] = jnp.zeros_like(caug_ref)
        m_ref[...] = jnp.zeros_like(m_ref)

    up_f, q_f, k_f, v_f, o_f, a_f = (r.at[front] for r in
                                     (up2_ref, q2_ref, k2_ref, v2_ref, o2_ref, a2_ref))
    up_b, q_b, k_b, v_b, o_b, a_b = (r.at[back] for r in
                                     (up2_ref, q2_ref, k2_ref, v2_ref, o2_ref, a2_ref))

    rb = MIX_ROW_BLOCK
    nrb = tile // rb
    rblocks = [slice(r * rb, (r + 1) * rb) for r in range(nrb)]

    def proj(rows, lo, hi):
        return jnp.dot(hb_ref[rows, :], win_ref[:, lo:hi], preferred_element_type=F32)

    def f_rms(rows):
        def fn():
            hb_ref[rows, :] = _rms(x_ref[0, rows, :], g_ref[...]).astype(BF16)
        return fn

    def f_u(rows):
        def fn():
            ub_ref[rows, :] = proj(rows, 0, ds).astype(BF16)
        return fn

    def f_perm(rows):
        def fn():
            up_f[rows, :] = jnp.dot(p_ref[rows, :], ub_ref[...],
                                    preferred_element_type=F32).astype(BF16)
        return fn

    def f_gates():
        gt = lax.dot_general(wg_ref[...], hb_ref[...], (((1,), (1,)), ((), ())),
                             preferred_element_type=F32) + gb_ref[...]
        _ml_gate_rows(gt, a_f)

    def f_qk(c, r):
        def fn():
            v = proj(rblocks[r], ds + c * ncol, ds + (c + 1) * ncol)
            for sl in range(ncol // LANES):
                xc_ref[c * (ncol // LANES) + sl, tail + r * rb:tail + (r + 1) * rb, :] = (
                    v[:, sl * LANES:(sl + 1) * LANES])
        return fn

    def f_v(rows):
        def fn():
            v_f[rows, :] = proj(rows, ds + 2 * dml, ds + 3 * dml).astype(BF16)
        return fn

    def f_o(rows):
        def fn():
            o_f[rows, :] = 1.0 + jnp.exp2(proj(rows, ds + 3 * dml, ds + 4 * dml) * (-LOG2E))
        return fn

    scans = [_s5_scan_pieces(j, ar_ref, ai_ref, asr_ref, asi_ref, bu_ref, xs_ref, xb_ref,
                             carry_ref, seg) for j in range(nblk)]

    def b_unperm(rows):
        def fn():
            y_ref[0, rows, :ds] = jnp.dot(pt_ref[rows, :], zn_ref[...],
                                          preferred_element_type=F32).astype(BF16)
        return fn

    def b_ml(hd):
        def fn():
            dh = dml // ML_HEADS
            y_ref[0, :, ds + hd * dh:ds + (hd + 1) * dh] = _ml_back(hd, o_b, h_ref, ng_ref)
        return fn

    nchunk = tile // ML_CHUNK
    all_rms = [f"rms{r}" for r in range(nrb)]
    mxu_stream = [(f"u{r}", 128, [f"rms{r}"], f_u(rblocks[r])) for r in range(nrb)]
    mxu_stream += [(f"perm{r}", 64, [f"u{q}" for q in range(nrb)], f_perm(rblocks[r]))
                   for r in range(nrb)]
    mxu_stream += [(f"qk0_{r}", 128, [f"rms{r}"], f_qk(0, r)) for r in range(nrb)]
    for c in range(nblk):
        if c + 1 < nblk:
            mxu_stream += [(f"qk{c + 1}_{r}", 128, [f"rms{r}"], f_qk(c + 1, r))
                           for r in range(nrb)]
        mxu_stream += [(f"bu{c}_{r}", 150, [f"perm{r}", scans[c][1]], functools.partial(
            _s5_bu, c, rblocks[r], up_f, bcat_ref, bu_ref)) for r in range(nrb)]
        if c == 0:
            mxu_stream += [(f"v{r}", 128, [f"rms{r}"], f_v(rblocks[r])) for r in range(nrb)]
        if c == 1:
            mxu_stream += [(f"o{r}", 256, [f"rms{r}"], f_o(rblocks[r])) for r in range(nrb)]

    vpu_stream = [(f"rms{r}", 150, [], f_rms(rblocks[r])) for r in range(nrb)]
    vpu_stream.append(("gates", 300, all_rms, f_gates))
    for j in range(nblk):
        vpu_stream.extend(scans[j][0])
        for part in range(nrb):
            vpu_stream.append((f"conv{j}_{part}", 275, [f"qk{j}_{part}"], functools.partial(
                _ml_conv, j, part, nrb, ncol, cw_ref, cb_ref, xc_ref, q_f, k_f)))
    _run_interleaved([mxu_stream, vpu_stream])

    chunk_stream = [(f"chunk{c}", 1500, [], functools.partial(
        _ml_chunk, c, q_b, k_b, v_b, o_b, a_b, h_ref, caug_ref, m_ref)) for c in range(nchunk)]
    chunk_stream += [(f"mlback{hd}", 300, [], b_ml(hd)) for hd in range(ML_HEADS)]
    out_stream = [(f"cproj{c}_{r}", 130, [], functools.partial(
        _s5_cproj, c, rblocks[r], xb_ref, ccat_ref, ys_ref))
        for c in range(nblk) for r in range(nrb)]
    out_stream += [(f"s5back{r}", 350, [], functools.partial(
        _s5_back, rblocks[r], ys_ref, up_b, d_ref, wglu_ref, bglu_ref, og_ref, zn_ref))
        for r in range(nrb)]
    out_stream += [(f"unperm{r}", 64, [], b_unperm(rblocks[r])) for r in range(nrb)]
    _run_interleaved([chunk_stream, out_stream])


def _mix_entry(*refs, **static):
    step = pl.program_id(0)
    for parity in range(2):
        pl.when(step % 2 == parity)(functools.partial(_mix_kernel, *refs, front=parity, **static))


def _layer_spec(arr, layer, block=None):
    block = tuple(arr.shape[1:]) if block is None else block
    nd = len(block)
    return pl.BlockSpec((None,) + block, lambda *_: (layer,) + (0,) * nd,
                        pipeline_mode=pl.Buffered(1))


def _mixer(x, layer, p):
    bsz, seq, dm = x.shape
    tile = MIX_TILE
    seg = tile // SUBLANES
    tiles_per_seq = seq // tile
    ntiles = bsz * tiles_per_seq
    ds = p["w_glu"].shape[2]
    dml = p["ml_norm_g"].shape[2]
    dh = dml // ML_HEADS
    nmain = ds + 4 * dml
    nst = p["ar"].shape[2]
    nblk = p["bcat"].shape[1]
    r = jnp.arange(tile)
    src = (r % SUBLANES) * seg + r // SUBLANES
    perm = (jnp.arange(tile)[None, :] == src[:, None]).astype(BF16)
    stacked = [p[k] for k in ("norm_mix_g", "w_in", "wg", "gb")]
    consts = [perm, perm.T]
    stacked2 = [p[k] for k in ("bcat", "ar", "ai", "asr", "asi", "ccat", "s5_d", "w_glu", "b_glu",
                               "s5_out_g", "conv_w", "conv_b", "ml_norm_g")]
    args = [x.reshape(ntiles, tile, dm)] + stacked + consts + stacked2
    in_specs = [pl.BlockSpec((1, tile, dm), lambda s: (jnp.minimum(s, ntiles - 1), 0, 0))]
    in_specs += [_layer_spec(a, layer, (dm, nmain) if a is p["w_in"] else None) for a in stacked]
    in_specs += [_const_spec(a.shape) for a in consts]
    in_specs += [_layer_spec(a, layer) for a in stacked2]
    y = pl.pallas_call(
        functools.partial(_mix_entry, seg=seg, tiles_per_seq=tiles_per_seq),
        grid=(ntiles + 1,),
        in_specs=in_specs,
        out_specs=pl.BlockSpec((1, tile, ds + dml), lambda s: (jnp.maximum(s - 1, 0), 0, 0)),
        out_shape=jax.ShapeDtypeStruct((ntiles, tile, ds + dml), BF16),
        scratch_shapes=[
            pltpu.VMEM((tile, dm), BF16),
            pltpu.VMEM((tile, ds), BF16),
            pltpu.VMEM((2, tile, ds), BF16),
            pltpu.VMEM((nblk, tile, 2 * nst // nblk), F32),
            pltpu.VMEM((nblk, tile, 2 * nst // nblk), F32),
            pltpu.VMEM((nblk, tile, 2 * nst // nblk), BF16),
            pltpu.VMEM((tile, ds), F32),
            pltpu.VMEM((tile, ds), BF16),
            pltpu.VMEM((2, nst), F32),
            pltpu.VMEM((2 * dml // LANES, tile + SUBLANES, LANES), F32),
            pltpu.VMEM((2, tile, dml), BF16),
            pltpu.VMEM((2, tile, dml), BF16),
            pltpu.VMEM((2, tile, dml), BF16),
            pltpu.VMEM((2, tile, dml), F32),
            pltpu.VMEM((tile, dml), F32),
            pltpu.VMEM((2, 3, tile // ML_CHUNK, 2 * ML_HEADS, ML_CHUNK), F32),
            pltpu.VMEM((ML_HEADS, dh, 2 * dh), F32),
            pltpu.VMEM((2 * ML_HEADS, LANES), F32),
        ],
        compiler_params=pltpu.CompilerParams(
            dimension_semantics=("arbitrary",), vmem_limit_bytes=VMEM_LIMIT),
        name="mixer",
    )(*args)
    return y.reshape(bsz, seq, ds + dml)


def _ffn_kernel(x_ref, y_ref, wo_ref, g_ref, w1_ref, w2_ref, fg_ref, o_ref, *, final):
    x1 = x_ref[...] + jnp.dot(y_ref[...], wo_ref[...], preferred_element_type=F32)
    hn = _rms(x1, g_ref[...]).astype(BF16)
    o_ref[...] = x1
    dff = w1_ref.shape[1]
    for c in range(dff // FFN_COL):
        cs = slice(c * FFN_COL, (c + 1) * FFN_COL)
        a = jnp.maximum(jnp.dot(hn, w1_ref[:, cs], preferred_element_type=F32), 0.0)
        o_ref[...] += jnp.dot((a * a).astype(BF16), w2_ref[cs, :], preferred_element_type=F32)
    if final:
        o_ref[...] = _rms(o_ref[...], fg_ref[...])


def _ffn_block(x2, y2, layer, p, final):
    n, dm = x2.shape
    tile = FFN_TILE
    stacked = [p[k] for k in ("w_out", "norm_ffn_g", "w_ff1", "w_ff2")]
    args = [x2, y2] + stacked + [p["final_norm_g"]]
    row = lambda w: pl.BlockSpec((tile, w), lambda i: (i, 0))
    in_specs = ([row(dm), row(y2.shape[1])] + [_layer_spec(a, layer) for a in stacked]
                + [_const_spec(p["final_norm_g"].shape)])
    return pl.pallas_call(
        functools.partial(_ffn_kernel, final=final),
        grid=(n // tile,),
        in_specs=in_specs,
        out_specs=row(dm),
        out_shape=jax.ShapeDtypeStruct((n, dm), F32),
        compiler_params=pltpu.CompilerParams(
            dimension_semantics=("arbitrary",), vmem_limit_bytes=VMEM_LIMIT),
        name="out_ffn",
    )(*args)


def kernel(x, norm_mix_g, w_in, s5_a_re, s5_a_im, s5_log_dt, s5_b_re, s5_b_im, s5_c_re, s5_c_im, s5_d, s5_w_glu, s5_b_glu, s5_out_g, ml_conv_w, ml_conv_b, ml_b_i, ml_b_f, ml_norm_g, w_out, norm_ffn_g, w_ff1, w_ff2, final_norm_g):
    bsz, seq, dm = x.shape
    depth = w_in.shape[0]
    ds = s5_w_glu.shape[2]
    dml = ml_norm_g.shape[1]
    nmain = ds + 4 * dml
    seg = MIX_TILE // SUBLANES
    bcat, ccat, ar, ai, asr, asi = jax.vmap(
        functools.partial(_s5_params, seg=seg))(s5_a_re, s5_a_im, s5_log_dt, s5_b_re, s5_b_im,
                                                s5_c_re, s5_c_im)
    row = lambda v: v.reshape(depth, 1, v.shape[-1])
    p = dict(
        norm_mix_g=row(norm_mix_g), w_in=w_in[:, :, :nmain].astype(BF16),
        wg=jnp.swapaxes(w_in[:, :, nmain:], 1, 2).astype(BF16),
        gb=jnp.concatenate([ml_b_i, ml_b_f], axis=1)[:, :, None],
        bcat=bcat, ar=ar, ai=ai, asr=asr, asi=asi, ccat=ccat, s5_d=row(s5_d.reshape(depth, ds)),
        w_glu=s5_w_glu.astype(BF16), b_glu=row(s5_b_glu), s5_out_g=row(s5_out_g),
        conv_w=ml_conv_w, conv_b=row(ml_conv_b), ml_norm_g=row(ml_norm_g),
        w_out=w_out.astype(BF16), norm_ffn_g=row(norm_ffn_g), w_ff1=w_ff1.astype(BF16),
        w_ff2=w_ff2.astype(BF16), final_norm_g=final_norm_g.reshape(1, dm))
    for l in range(depth):
        y = _mixer(x, l, p)
        x = _ffn_block(x.reshape(bsz * seq, dm), y.reshape(bsz * seq, y.shape[2]), l, p,
                       final=(l == depth - 1)).reshape(bsz, seq, dm)
    return x
```

```python
import functools
import math

import jax
import jax.numpy as jnp
from jax import lax
from jax.experimental import pallas as pl
from jax.experimental.pallas import tpu as pltpu

EPS = 1e-6
SSM_GROUP = 16
SSM_STATE = 64
ML_HEADS = 4
CONV_WIDTH = 4

SUBLANES = 8
LANES = 128
V7X_VMEM_BYTES = 64 * 1024 * 1024
VMEM_LIMIT = V7X_VMEM_BYTES - 8 * 1024 * 1024

MIX_TILE = 512
MIX_ROW_BLOCK = 128
ML_CHUNK = LANES
FFN_TILE = 512
FFN_COL = 1024
S5_BLOCK_GROUPS = 8
S5_SCAN_PIECE = 16

F32 = jnp.float32
BF16 = jnp.bfloat16


def _rms(x, g):
    ms = jnp.mean(x * x, axis=-1, keepdims=True)
    return x * lax.rsqrt(ms + EPS) * g


LOG2E = math.log2(math.e)


def _sigmoid(x):
    return 1.0 / (1.0 + jnp.exp2(x * (-LOG2E)))


def _log_sigmoid(x):
    return jnp.minimum(x, 0.0) - jnp.log(1.0 + jnp.exp(-jnp.abs(x)))


def _gelu_tanh(x):
    k = 2.0 * math.sqrt(2.0 / math.pi) * LOG2E
    return x / (1.0 + jnp.exp2(x * (-k - (0.044715 * k) * (x * x))))


def _const_spec(shape):
    nd = len(shape)
    return pl.BlockSpec(shape, lambda *_: (0,) * nd, pipeline_mode=pl.Buffered(1))


def _run_interleaved(streams):
    totals = [sum(p[1] for p in s) for s in streams]
    pos = [0] * len(streams)
    spent = [0.0] * len(streams)
    done = set()
    while any(pos[i] < len(s) for i, s in enumerate(streams)):
        ready = [i for i, s in enumerate(streams)
                 if pos[i] < len(s) and all(r in done for r in s[pos[i]][2])]
        assert ready, "interleaving deadlock"
        i = min(ready, key=lambda k: spent[k] / totals[k])
        name, cost, _, fn = streams[i][pos[i]]
        fn()
        done.add(name)
        spent[i] += cost
        pos[i] += 1


def _s5_bu(j, rows, up_ref, bcat_ref, bu_ref):
    hw2 = bcat_ref.shape[2]
    gw = up_ref.shape[1] // bcat_ref.shape[0]
    bu_ref[j, rows, :] = jnp.dot(
        up_ref[rows, j * gw:(j + 1) * gw], bcat_ref[j], preferred_element_type=F32)


def _s5_scan_pieces(j, ar_ref, ai_ref, asr_ref, asi_ref, bu_ref, xs_ref, xb_ref, carry_ref, seg):
    nst = ar_ref.shape[1]
    hw = nst // (nst // (S5_BLOCK_GROUPS * SSM_STATE))
    cr = slice(0, hw)
    ci = slice(hw, 2 * hw)
    blk = slice(j * hw, (j + 1) * hw)
    pair = 2 * SUBLANES
    st = {}

    def coeffs():
        return (jnp.broadcast_to(ar_ref[:, blk], (SUBLANES, hw)),
                jnp.broadcast_to(ai_ref[:, blk], (SUBLANES, hw)))

    def pass1(s0):
        def fn():
            ar, ai = coeffs()
            sr, si = st.get("s", (jnp.zeros((SUBLANES, hw), F32),) * 2)
            for s in range(s0, s0 + S5_SCAN_PIECE):
                rows = slice(s * SUBLANES, (s + 1) * SUBLANES)
                sr, si = (ar * sr - ai * si + bu_ref[j, rows, cr],
                          ar * si + ai * sr + bu_ref[j, rows, ci])
                xs_ref[j, rows, cr] = sr
                xs_ref[j, rows, ci] = si
            st["s"] = (sr, si)
        return fn

    def carry():
        fr, fi = st["s"]
        asr = asr_ref[:, blk]
        asi = asi_ref[:, blk]
        rows_r = [carry_ref[0:1, blk]]
        rows_i = [carry_ref[1:2, blk]]
        for k in range(SUBLANES):
            pr, pi = rows_r[-1], rows_i[-1]
            rows_r.append(asr * pr - asi * pi + fr[k:k + 1])
            rows_i.append(asr * pi + asi * pr + fi[k:k + 1])
        carry_ref[0:1, blk] = rows_r[SUBLANES]
        carry_ref[1:2, blk] = rows_i[SUBLANES]
        st["w"] = (jnp.concatenate(rows_r[:SUBLANES], axis=0),
                   jnp.concatenate(rows_i[:SUBLANES], axis=0))

    def pass2(s0):
        def fn():
            ar, ai = coeffs()
            wr, wi = st["w"]
            for s in range(s0, s0 + S5_SCAN_PIECE, 2):
                r0 = slice(s * SUBLANES, (s + 1) * SUBLANES)
                r1 = slice((s + 1) * SUBLANES, (s + 2) * SUBLANES)
                w1r, w1i = ar * wr - ai * wi, ar * wi + ai * wr
                wr, wi = ar * w1r - ai * w1i, ar * w1i + ai * w1r
                both = slice(s * SUBLANES, (s + 2) * SUBLANES)
                xb_ref[j, both, cr] = jnp.concatenate(
                    [xs_ref[j, r0, cr] + w1r, xs_ref[j, r1, cr] + wr], axis=0).astype(BF16)
                xb_ref[j, both, ci] = jnp.concatenate(
                    [xs_ref[j, r0, ci] + w1i, xs_ref[j, r1, ci] + wi], axis=0).astype(BF16)
            st["w"] = (wr, wi)
        return fn

    pieces = []
    prev = []
    n = seg // S5_SCAN_PIECE
    for q in range(n):
        name = f"p1_{j}_{q}"
        pieces.append((name, 150, prev, pass1(q * S5_SCAN_PIECE)))
        prev = [name]
    pieces.append((f"carry{j}", 60, prev, carry))
    prev = [f"carry{j}"]
    for q in range(n):
        name = f"p2_{j}_{q}"
        pieces.append((name, 150, prev, pass2(q * S5_SCAN_PIECE)))
        prev = [name]
    return pieces, f"p1_{j}_{n - 1}", f"p2_{j}_{n - 1}"


def _s5_cproj(j, rows, xb_ref, ccat_ref, ys_ref):
    hw2 = ccat_ref.shape[1]
    gw = ccat_ref.shape[2]
    ys_ref[rows, j * gw:(j + 1) * gw] = jnp.dot(xb_ref[j, rows, :],
                                                ccat_ref[j], preferred_element_type=F32)


def _s5_back(rows, ys_ref, up_ref, d_ref, wglu_ref, bglu_ref, og_ref, zn_ref):
    y = ys_ref[rows, :] + d_ref[...] * up_ref[rows, :].astype(F32)
    z = _gelu_tanh(y)
    gate = _sigmoid(jnp.dot(z.astype(BF16), wglu_ref[...], preferred_element_type=F32)
                    + bglu_ref[...])
    zn_ref[rows, :] = _rms(z * gate, og_ref[...]).astype(BF16)


def _s5_params(a_re, a_im, log_dt, b_re, b_im, c_re, c_im, seg):
    dt = jnp.exp(log_dt)[:, None]
    mag = jnp.exp(a_re * dt)
    ab_re = mag * jnp.cos(a_im * dt)
    ab_im = mag * jnp.sin(a_im * dt)
    den = jnp.square(a_re) + jnp.square(a_im)
    zr = ab_re - 1.0
    s_re = (zr * a_re + ab_im * a_im) / den
    s_im = (ab_im * a_re - zr * a_im) / den
    bb_re = s_re[..., None] * b_re - s_im[..., None] * b_im
    bb_im = s_re[..., None] * b_im + s_im[..., None] * b_re
    mag_s = jnp.exp(a_re * dt * seg)
    as_re = mag_s * jnp.cos(a_im * dt * seg)
    as_im = mag_s * jnp.sin(a_im * dt * seg)

    g, n, p = bb_re.shape
    gb = S5_BLOCK_GROUPS
    nblk = g // gb
    eye = jnp.eye(gb, dtype=F32)

    def in_blocks(bb):
        t = bb.reshape(nblk, gb, n, p)
        m = jnp.einsum('jgnp,gh->jgphn', t, eye)
        return m.reshape(nblk, gb * p, gb * n)

    def out_blocks(cc):
        t = cc.reshape(nblk, gb, p, n)
        m = jnp.einsum('jgpn,gh->jgnhp', t, eye)
        return m.reshape(nblk, gb * n, gb * p)

    bcat = jnp.concatenate([in_blocks(bb_re), in_blocks(bb_im)], axis=2).astype(BF16)
    ccat = jnp.concatenate([out_blocks(c_re), -out_blocks(c_im)], axis=1).astype(BF16)
    flat = lambda v: v.reshape(1, g * n)
    return bcat, ccat, flat(ab_re), flat(ab_im), flat(as_re), flat(as_im)


def _ml_conv(c, part, nparts, ncol, cw_ref, cb_ref, xc_ref, q_ref, k_ref):
    tile, dml = q_ref.shape
    dh = dml // ML_HEADS
    tail = SUBLANES
    nrow = tile // nparts
    r0 = part * nrow
    rows = slice(r0, r0 + nrow)
    for sl in range(ncol // LANES):
        slab = c * (ncol // LANES) + sl
        cols = slice(slab * LANES, (slab + 1) * LANES)
        acc = cb_ref[:, cols]
        for j in range(CONV_WIDTH):
            off = r0 + tail - (CONV_WIDTH - 1) + j
            acc = acc + cw_ref[j:j + 1, cols] * xc_ref[slab, pl.ds(off, nrow, stride=1), :]
        if part == nparts - 1:
            xc_ref[slab, 0:tail, :] = xc_ref[slab, tile:tile + tail, :]
        qk = acc / (1.0 + jnp.exp2(acc * (-LOG2E)))
        if cols.stop <= dml:
            q_ref[rows, cols] = qk.astype(BF16)
        else:
            k_ref[rows, cols.start - dml:cols.stop - dml] = (
                qk * (1.0 / math.sqrt(dh))).astype(BF16)


def _ml_gate_rows(gt, a_ref):
    tile = gt.shape[1]
    chunk = ML_CHUNK
    lf = _log_sigmoid(gt)
    lane = lax.broadcasted_iota(jnp.int32, gt.shape, 1) % chunk
    neg_inf = jnp.float32(-jnp.inf)
    bc = lf
    shift = 1
    while shift < chunk:
        bc = bc + jnp.where(lane >= shift, pltpu.roll(bc, shift, axis=1), 0.0)
        shift *= 2
    a8 = gt - pltpu.roll(bc, ML_HEADS, axis=0)
    cm = a8
    shift = 1
    while shift < chunk:
        cm = jnp.maximum(cm, jnp.where(lane >= shift, pltpu.roll(cm, shift, axis=1), neg_inf))
        shift *= 2
    for c in range(tile // chunk):
        cs = slice(c * chunk, (c + 1) * chunk)
        a_ref[0, c] = a8[:, cs]
        a_ref[1, c] = cm[:, cs]
        a_ref[2, c] = bc[:, cs]


def _ml_chunk(c, q_ref, k_ref, v_ref, o_ref, a_ref, h_ref, caug_ref, m_ref):
    tile, dml = q_ref.shape
    dh = dml // ML_HEADS
    chunk = ML_CHUNK
    neg_inf = jnp.float32(-jnp.inf)
    row_i = lax.broadcasted_iota(jnp.int32, (chunk, chunk), 0)
    col_i = lax.broadcasted_iota(jnp.int32, (chunk, chunk), 1)
    tril = col_i <= row_i
    ones_blk = jnp.ones((chunk, dh), BF16)

    def col_bcast(row):
        return jnp.broadcast_to(row, (chunk, chunk)).T

    rows = slice(c * chunk, (c + 1) * chunk)
    a_t, cm_t, bc_t = a_ref[0, c], a_ref[1, c], a_ref[2, c]
    for hd in range(ML_HEADS):
        sl = slice(hd * dh, (hd + 1) * dh)
        qh = q_ref[rows, sl]
        kh = k_ref[rows, sl]
        vh = v_ref[rows, sl]
        a_r = a_t[hd:hd + 1, :]
        cm_r = cm_t[hd:hd + 1, :]
        bc_r = bc_t[ML_HEADS + hd:ML_HEADS + hd + 1, :]
        m_prev = m_ref[hd:hd + 1, :]
        m_r = jnp.maximum(cm_r, m_prev)
        m_col = col_bcast(m_r)
        d_in = jnp.exp(jnp.where(tril, a_r - m_col, neg_inf))
        d_st = col_bcast(jnp.exp(m_prev - m_r))
        e = col_bcast(jnp.exp(-(bc_r + m_r)))
        s = lax.dot_general(qh, kh, (((1,), (1,)), ((), ())),
                            preferred_element_type=F32) * d_in
        lhs = jnp.concatenate([s, qh.astype(F32) * d_st], axis=1).astype(BF16)
        vaug = jnp.concatenate([vh, ones_blk], axis=1)
        caug = caug_ref[hd]
        rhs = jnp.concatenate([vaug, caug.astype(BF16)], axis=0)
        na = jnp.dot(lhs, rhs, preferred_element_type=F32)
        h_ref[rows, sl] = na[:, :dh] / (jnp.maximum(jnp.abs(na[:, dh:]), e) * o_ref[rows, sl])
        wk_r = d_in[chunk - 1:chunk, :]
        decay = d_st[chunk - 1:chunk, :]
        kwt = (kh.T.astype(F32) * wk_r).astype(BF16)
        upd = jnp.dot(kwt, vaug, preferred_element_type=F32)
        caug_ref[hd] = jnp.concatenate([decay, decay], axis=1) * caug + upd
        m_ref[hd:hd + 1, :] = (bc_r[:, chunk - 1:chunk]
                               + jnp.maximum(cm_r[:, chunk - 1:chunk], m_prev))


def _ml_back(hd, o_ref, h_ref, ng_ref):
    dh = o_ref.shape[1] // ML_HEADS
    sl = slice(hd * dh, (hd + 1) * dh)
    del o_ref
    return _rms(h_ref[:, sl], ng_ref[:, sl]).astype(BF16)


def _mix_kernel(x_ref, g_ref, win_ref, wg_ref, gb_ref,
                p_ref, pt_ref, bcat_ref, ar_ref, ai_ref, asr_ref, asi_ref, ccat_ref, d_ref,
                wglu_ref, bglu_ref, og_ref, cw_ref, cb_ref, ng_ref,
                y_ref,
                hb_ref, ub_ref, up2_ref, bu_ref, xs_ref, xb_ref, ys_ref, zn_ref, carry_ref,
                xc_ref, q2_ref, k2_ref, v2_ref, o2_ref, h_ref, a2_ref, caug_ref, m_ref,
                *, seg, tiles_per_seq, front):
    tile = hb_ref.shape[0]
    ds = up2_ref.shape[2]
    dml = q2_ref.shape[2]
    nblk = bcat_ref.shape[0]
    ncol = 2 * dml // nblk
    tail = SUBLANES
    step = pl.program_id(0)
    back = 1 - front

    @pl.when(step == 0)
    def _():
        up2_ref[...] = jnp.zeros_like(up2_ref)
        bu_ref[...] = jnp.zeros_like(bu_ref)
        q2_ref[...] = jnp.zeros_like(q2_ref)
        k2_ref[...] = jnp.zeros_like(k2_ref)
        v2_ref[...] = jnp.zeros_like(v2_ref)
        o2_ref[...] = jnp.ones_like(o2_ref)
        a2_ref[...] = jnp.zeros_like(a2_ref)
        carry_ref[...] = jnp.zeros_like(carry_ref)
        caug_ref[...] = jnp.zeros_like(caug_ref)
        m_ref[...] = jnp.zeros_like(m_ref)

    @pl.when(step % tiles_per_seq == 0)
    def _():
        xc_ref[:, 0:tail, :] = jnp.zeros((xc_ref.shape[0], tail, LANES), F32)

    @pl.when(step % tiles_per_seq == 1 % tiles_per_seq)
    def _():
        carry_ref[...] = jnp.zeros_like(carry_ref)
        caug_ref[...] = jnp.zeros_like(caug_ref)
        m_ref[...] = jnp.zeros_like(m_ref)

    up_f, q_f, k_f, v_f, o_f, a_f = (r.at[front] for r in
                                     (up2_ref, q2_ref, k2_ref, v2_ref, o2_ref, a2_ref))
    up_b, q_b, k_b, v_b, o_b, a_b = (r.at[back] for r in
                                     (up2_ref, q2_ref, k2_ref, v2_ref, o2_ref, a2_ref))

    rb = MIX_ROW_BLOCK
    nrb = tile // rb
    rblocks = [slice(r * rb, (r + 1) * rb) for r in range(nrb)]

    def proj(rows, lo, hi):
        return jnp.dot(hb_ref[rows, :], win_ref[:, lo:hi], preferred_element_type=F32)

    def f_rms(rows):
        def fn():
            hb_ref[rows, :] = _rms(x_ref[0, rows, :], g_ref[...]).astype(BF16)
        return fn

    def f_u(rows):
        def fn():
            ub_ref[rows, :] = proj(rows, 0, ds).astype(BF16)
        return fn

    def f_perm(rows):
        def fn():
            up_f[rows, :] = jnp.dot(p_ref[rows, :], ub_ref[...],
                                    preferred_element_type=F32).astype(BF16)
        return fn

    def f_gates():
        gt = lax.dot_general(wg_ref[...], hb_ref[...], (((1,), (1,)), ((), ())),
                             preferred_element_type=F32) + gb_ref[...]
        _ml_gate_rows(gt, a_f)

    def f_qk(c, r):
        def fn():
            v = proj(rblocks[r], ds + c * ncol, ds + (c + 1) * ncol)
            for sl in range(ncol // LANES):
                xc_ref[c * (ncol // LANES) + sl, tail + r * rb:tail + (r + 1) * rb, :] = (
                    v[:, sl * LANES:(sl + 1) * LANES])
        return fn

    def f_v(rows):
        def fn():
            v_f[rows, :] = proj(rows, ds + 2 * dml, ds + 3 * dml).astype(BF16)
        return fn

    def f_o(rows):
        def fn():
            o_f[rows, :] = 1.0 + jnp.exp2(proj(rows, ds + 3 * dml, ds + 4 * dml) * (-LOG2E))
        return fn

    scans = [_s5_scan_pieces(j, ar_ref, ai_ref, asr_ref, asi_ref, bu_ref, xs_ref, xb_ref,
                             carry_ref, seg) for j in range(nblk)]

    def b_unperm(rows):
        def fn():
            y_ref[0, rows, :ds] = jnp.dot(pt_ref[rows, :], zn_ref[...],
                                          preferred_element_type=F32).astype(BF16)
        return fn

    def b_ml(hd):
        def fn():
            dh = dml // ML_HEADS
            y_ref[0, :, ds + hd * dh:ds + (hd + 1) * dh] = _ml_back(hd, o_b, h_ref, ng_ref)
        return fn

    nchunk = tile // ML_CHUNK
    all_rms = [f"rms{r}" for r in range(nrb)]
    mxu_stream = [(f"u{r}", 128, [f"rms{r}"], f_u(rblocks[r])) for r in range(nrb)]
    mxu_stream += [(f"perm{r}", 64, [f"u{q}" for q in range(nrb)], f_perm(rblocks[r]))
                   for r in range(nrb)]
    mxu_stream += [(f"qk0_{r}", 128, [f"rms{r}"], f_qk(0, r)) for r in range(nrb)]
    for c in range(nblk):
        if c + 1 < nblk:
            mxu_stream += [(f"qk{c + 1}_{r}", 128, [f"rms{r}"], f_qk(c + 1, r))
                           for r in range(nrb)]
        mxu_stream += [(f"bu{c}_{r}", 150, [f"perm{r}", scans[c][1]], functools.partial(
            _s5_bu, c, rblocks[r], up_f, bcat_ref, bu_ref)) for r in range(nrb)]
        if c == 0:
            mxu_stream += [(f"v{r}", 128, [f"rms{r}"], f_v(rblocks[r])) for r in range(nrb)]
        if c == 1:
            mxu_stream += [(f"o{r}", 256, [f"rms{r}"], f_o(rblocks[r])) for r in range(nrb)]

    vpu_stream = [(f"rms{r}", 150, [], f_rms(rblocks[r])) for r in range(nrb)]
    vpu_stream.append(("gates", 300, all_rms, f_gates))
    for j in range(nblk):
        vpu_stream.extend(scans[j][0])
        for part in range(nrb):
            vpu_stream.append((f"conv{j}_{part}", 275, [f"qk{j}_{part}"], functools.partial(
                _ml_conv, j, part, nrb, ncol, cw_ref, cb_ref, xc_ref, q_f, k_f)))
    _run_interleaved([mxu_stream, vpu_stream])

    chunk_stream = [(f"chunk{c}", 1500, [], functools.partial(
        _ml_chunk, c, q_b, k_b, v_b, o_b, a_b, h_ref, caug_ref, m_ref)) for c in range(nchunk)]
    chunk_stream += [(f"mlback{hd}", 300, [], b_ml(hd)) for hd in range(ML_HEADS)]
    out_stream = [(f"cproj{c}_{r}", 130, [], functools.partial(
        _s5_cproj, c, rblocks[r], xb_ref, ccat_ref, ys_ref))
        for c in range(nblk) for r in range(nrb)]
    out_stream += [(f"s5back{r}", 350, [], functools.partial(
        _s5_back, rblocks[r], ys_ref, up_b, d_ref, wglu_ref, bglu_ref, og_ref, zn_ref))
        for r in range(nrb)]
    out_stream += [(f"unperm{r}", 64, [], b_unperm(rblocks[r])) for r in range(nrb)]
    _run_interleaved([chunk_stream, out_stream])


def _mix_entry(*refs, **static):
    step = pl.program_id(0)
    for parity in range(2):
        pl.when(step % 2 == parity)(functools.partial(_mix_kernel, *refs, front=parity, **static))


def _layer_spec(arr, layer, block=None):
    block = tuple(arr.shape[1:]) if block is None else block
    nd = len(block)
    return pl.BlockSpec((None,) + block, lambda *_: (layer,) + (0,) * nd,
                        pipeline_mode=pl.Buffered(1))


def _mixer(x, layer, p):
    bsz, seq, dm = x.shape
    tile = MIX_TILE
    seg = tile // SUBLANES
    tiles_per_seq = seq // tile
    ntiles = bsz * tiles_per_seq
    ds = p["w_glu"].shape[2]
    dml = p["ml_norm_g"].shape[2]
    dh = dml // ML_HEADS
    nmain = ds + 4 * dml
    nst = p["ar"].shape[2]
    nblk = p["bcat"].shape[1]
    r = jnp.arange(tile)
    src = (r % SUBLANES) * seg + r // SUBLANES
    perm = (jnp.arange(tile)[None, :] == src[:, None]).astype(BF16)
    stacked = [p[k] for k in ("norm_mix_g", "w_in", "wg", "gb")]
    consts = [perm, perm.T]
    stacked2 = [p[k] for k in ("bcat", "ar", "ai", "asr", "asi", "ccat", "s5_d", "w_glu", "b_glu",
                               "s5_out_g", "conv_w", "conv_b", "ml_norm_g")]
    args = [x.reshape(ntiles, tile, dm)] + stacked + consts + stacked2
    in_specs = [pl.BlockSpec((1, tile, dm), lambda s: (jnp.minimum(s, ntiles - 1), 0, 0))]
    in_specs += [_layer_spec(a, layer, (dm, nmain) if a is p["w_in"] else None) for a in stacked]
    in_specs += [_const_spec(a.shape) for a in consts]
    in_specs += [_layer_spec(a, layer) for a in stacked2]
    y = pl.pallas_call(
        functools.partial(_mix_entry, seg=seg, tiles_per_seq=tiles_per_seq),
        grid=(ntiles + 1,),
        in_specs=in_specs,
        out_specs=pl.BlockSpec((1, tile, ds + dml), lambda s: (jnp.maximum(s - 1, 0), 0, 0)),
        out_shape=jax.ShapeDtypeStruct((ntiles, tile, ds + dml), BF16),
        scratch_shapes=[
            pltpu.VMEM((tile, dm), BF16),
            pltpu.VMEM((tile, ds), BF16),
            pltpu.VMEM((2, tile, ds), BF16),
            pltpu.VMEM((nblk, tile, 2 * nst // nblk), F32),
            pltpu.VMEM((nblk, tile, 2 * nst // nblk), F32),
            pltpu.VMEM((nblk, tile, 2 * nst // nblk), BF16),
            pltpu.VMEM((tile, ds), F32),
            pltpu.VMEM((tile, ds), BF16),
            pltpu.VMEM((2, nst), F32),
            pltpu.VMEM((2 * dml // LANES, tile + SUBLANES, LANES), F32),
            pltpu.VMEM((2, tile, dml), BF16),
            pltpu.VMEM((2, tile, dml), BF16),
            pltpu.VMEM((2, tile, dml), BF16),
            pltpu.VMEM((2, tile, dml), F32),
            pltpu.VMEM((tile, dml), F32),
            pltpu.VMEM((2, 3, tile // ML_CHUNK, 2 * ML_HEADS, ML_CHUNK), F32),
            pltpu.VMEM((ML_HEADS, dh, 2 * dh), F32),
            pltpu.VMEM((2 * ML_HEADS, LANES), F32),
        ],
        compiler_params=pltpu.CompilerParams(
            dimension_semantics=("arbitrary",), vmem_limit_bytes=VMEM_LIMIT),
        name="mixer",
    )(*args)
    return y.reshape(bsz, seq, ds + dml)


def _ffn_kernel(x_ref, y_ref, wo_ref, g_ref, w1_ref, w2_ref, fg_ref, o_ref, *, final):
    x1 = x_ref[...] + jnp.dot(y_ref[...], wo_ref[...], preferred_element_type=F32)
    hn = _rms(x1, g_ref[...]).astype(BF16)
    o_ref[...] = x1
    dff = w1_ref.shape[1]
    for c in range(dff // FFN_COL):
        cs = slice(c * FFN_COL, (c + 1) * FFN_COL)
        a = jnp.maximum(jnp.dot(hn, w1_ref[:, cs], preferred_element_type=F32), 0.0)
        o_ref[...] += jnp.dot((a * a).astype(BF16), w2_ref[cs, :], preferred_element_type=F32)
    if final:
        o_ref[...] = _rms(o_ref[...], fg_ref[...])


def _ffn_block(x2, y2, layer, p, final):
    n, dm = x2.shape
    tile = FFN_TILE
    stacked = [p[k] for k in ("w_out", "norm_ffn_g", "w_ff1", "w_ff2")]
    args = [x2, y2] + stacked + [p["final_norm_g"]]
    row = lambda w: pl.BlockSpec((tile, w), lambda i: (i, 0))
    in_specs = ([row(dm), row(y2.shape[1])] + [_layer_spec(a, layer) for a in stacked]
                + [_const_spec(p["final_norm_g"].shape)])
    return pl.pallas_call(
        functools.partial(_ffn_kernel, final=final),
        grid=(n // tile,),
        in_specs=in_specs,
        out_specs=row(dm),
        out_shape=jax.ShapeDtypeStruct((n, dm), F32),
        compiler_params=pltpu.CompilerParams(
            dimension_semantics=("arbitrary",), vmem_limit_bytes=VMEM_LIMIT),
        name="out_ffn",
    )(*args)


def kernel(x, norm_mix_g, w_in, s5_a_re, s5_a_im, s5_log_dt, s5_b_re, s5_b_im, s5_c_re, s5_c_im, s5_d, s5_w_glu, s5_b_glu, s5_out_g, ml_conv_w, ml_conv_b, ml_b_i, ml_b_f, ml_norm_g, w_out, norm_ffn_g, w_ff1, w_ff2, final_norm_g):
    bsz, seq, dm = x.shape
    depth = w_in.shape[0]
    ds = s5_w_glu.shape[2]
    dml = ml_norm_g.shape[1]
    nmain = ds + 4 * dml
    seg = MIX_TILE // SUBLANES
    bcat, ccat, ar, ai, asr, asi = jax.vmap(
        functools.partial(_s5_params, seg=seg))(s5_a_re, s5_a_im, s5_log_dt, s5_b_re, s5_b_im,
                                                s5_c_re, s5_c_im)
    row = lambda v: v.reshape(depth, 1, v.shape[-1])
    p = dict(
        norm_mix_g=row(norm_mix_g), w_in=w_in[:, :, :nmain].astype(BF16),
        wg=jnp.swapaxes(w_in[:, :, nmain:], 1, 2).astype(BF16),
        gb=jnp.concatenate([ml_b_i, ml_b_f], axis=1)[:, :, None],
        bcat=bcat, ar=ar, ai=ai, asr=asr, asi=asi, ccat=ccat, s5_d=row(s5_d.reshape(depth, ds)),
        w_glu=s5_w_glu.astype(BF16), b_glu=row(s5_b_glu), s5_out_g=row(s5_out_g),
        conv_w=ml_conv_w, conv_b=row(ml_conv_b), ml_norm_g=row(ml_norm_g),
        w_out=w_out.astype(BF16), norm_ffn_g=row(norm_ffn_g), w_ff1=w_ff1.astype(BF16),
        w_ff2=w_ff2.astype(BF16), final_norm_g=final_norm_g.reshape(1, dm))
    for l in range(depth):
        y = _mixer(x, l, p)
        x = _ffn_block(x.reshape(bsz * seq, dm), y.reshape(bsz * seq, y.shape[2]), l, p,
                       final=(l == depth - 1)).reshape(bsz, seq, dm)
    return x
```
